```python
import math
import jax
import jax.numpy as jnp
from jax import lax
import numpy as np

D_MODEL = 1024
BATCH = 8
SEQ = 2048
DEPTH = 2
DEC_BATCH = 16
DEC_SEQ = 2048
PAST_LEN = 128

N_HEADS = 8
N_KV_HEADS = 2
HEAD_DIM = 64
Q_GROUP = N_HEADS // N_KV_HEADS
ATTN_WIDTH = N_HEADS * HEAD_DIM
KV_WIDTH = N_KV_HEADS * HEAD_DIM
WINDOW = 128
BLOCK = 128
ROPE_DIM = HEAD_DIM // 4
ROPE_THETA = 500000.0
NEG_INF = -1e30

HYENA_WIDTH = D_MODEL // 2
HYENA_ORDER = 2
SHORT_CONV = 3
POS_FREQS = 16
POS_EMB = 1 + 2 * POS_FREQS
FILTER_HIDDEN = 64
FAST_DECAY_PCT = 0.3
SLOW_DECAY_PCT = 1.5
DECAY_TARGET = 0.01
DECAY_SHIFT = 0.05

IN_WIDTH = ATTN_WIDTH + 2 * KV_WIDTH + (HYENA_ORDER + 1) * HYENA_WIDTH
MIX_WIDTH = ATTN_WIDTH + HYENA_WIDTH

CONV_INNER = D_MODEL
CONV_KERNEL = 31

N_EXPERTS = 64
N_GROUPS = 8
EXPERTS_PER_GROUP = N_EXPERTS // N_GROUPS
TOP_K = 2
D_EXPERT = 512
MOE_BLOCK = 128

N_MOD = 6
N_EVEN = (DEPTH + 1) // 2
N_ODD = DEPTH // 2
EPS = 1e-6

kernel_name = 'hybrid_swa_hyena_conformer_grouped_moe_adaln'


def rms_norm(x, g):
    xf = x.astype(jnp.float32)
    y = xf * lax.rsqrt(jnp.mean(xf * xf, axis=-1, keepdims=True) + EPS)
    return (y * g.astype(jnp.float32)).astype(x.dtype)


def layer_norm(x, g, b):
    xf = x.astype(jnp.float32)
    mu = jnp.mean(xf, axis=-1, keepdims=True)
    var = jnp.mean(jnp.square(xf - mu), axis=-1, keepdims=True)
    y = (xf - mu) * lax.rsqrt(var + EPS)
    return (y * g.astype(jnp.float32) + b.astype(jnp.float32)).astype(x.dtype)


def modulate(h, shift, scale):
    return h * (1 + scale[:, None, :]) + shift[:, None, :]


def partial_rope(x, pos):
    half = ROPE_DIM // 2
    inv_freq = ROPE_THETA ** (-jnp.arange(half, dtype=jnp.float32) * 2.0 / ROPE_DIM)
    ang = pos.astype(jnp.float32)[:, None] * inv_freq[None, :]
    cos = jnp.cos(ang)[None, :, None, :]
    sin = jnp.sin(ang)[None, :, None, :]
    xr = x[..., :ROPE_DIM].astype(jnp.float32)
    x1, x2 = xr[..., :half], xr[..., half:]
    rot = jnp.concatenate([x1 * cos - x2 * sin, x2 * cos + x1 * sin], axis=-1)
    return jnp.concatenate([rot.astype(x.dtype), x[..., ROPE_DIM:]], axis=-1)


def depthwise_conv(x, w, b):
    k = w.shape[0]
    y = lax.conv_general_dilated(x, w[:, None, :].astype(x.dtype), window_strides=(1,),
                                 padding=[(k // 2, k // 2)],
                                 dimension_numbers=('NWC', 'WIO', 'NWC'),
                                 feature_group_count=x.shape[-1])
    return y + b


def window_attention(q, k, v, sink):
    b, l = q.shape[0], q.shape[1]
    nb = l // BLOCK
    qb = q.reshape(b, nb, BLOCK, N_KV_HEADS, Q_GROUP, HEAD_DIM)

    def bands(t):
        tp = jnp.pad(t, ((0, 0), (BLOCK, BLOCK), (0, 0), (0, 0)))
        tp = tp.reshape(b, nb + 2, BLOCK, N_KV_HEADS, HEAD_DIM)
        return jnp.concatenate([tp[:, :-2], tp[:, 1:-1], tp[:, 2:]], axis=2)

    kb, vb = bands(k), bands(v)
    s = jnp.einsum('bnqkgd,bnskd->bnkgqs', qb, kb,
                   preferred_element_type=jnp.float32) * (HEAD_DIM ** -0.5)
    blk = jnp.arange(nb)[:, None] * BLOCK
    qpos = blk + jnp.arange(BLOCK)[None, :]
    kpos = blk - BLOCK + jnp.arange(3 * BLOCK)[None, :]
    rel = kpos[:, None, :] - qpos[:, :, None]
    valid = (jnp.abs(rel) <= WINDOW) & (kpos[:, None, :] >= 0) & (kpos[:, None, :] < l)
    s = jnp.where(valid[None, :, None, None], s, NEG_INF)
    sk = sink.astype(jnp.float32).reshape(N_KV_HEADS, Q_GROUP)[None, None, :, :, None, None]
    m = jnp.maximum(jnp.max(s, axis=-1, keepdims=True), sk)
    e = jnp.exp(s - m)
    p = e / (jnp.sum(e, axis=-1, keepdims=True) + jnp.exp(sk - m))
    o = jnp.einsum('bnkgqs,bnskd->bnqkgd', p.astype(v.dtype), vb)
    return o.reshape(b, l, ATTN_WIDTH)


def hyena_filters(l, w1, b1, fr1, w2, b2, fr2, w3):
    f32 = jnp.float32
    t = jnp.arange(l, dtype=f32) / l
    freqs = jnp.arange(1, POS_FREQS + 1, dtype=f32)
    ang = 2.0 * math.pi * t[:, None] * freqs[None, :]
    z = jnp.concatenate([t[:, None], jnp.cos(ang), jnp.sin(ang)], axis=-1)
    hdn = jnp.sin(fr1.astype(f32) * (z @ w1.astype(f32) + b1.astype(f32)))
    hdn = jnp.sin(fr2.astype(f32) * (hdn @ w2.astype(f32) + b2.astype(f32)))
    hf = (hdn @ w3.astype(f32)).reshape(l, HYENA_ORDER, 2, HYENA_WIDTH)
    deltas = jnp.abs(jnp.linspace(math.log(DECAY_TARGET) / FAST_DECAY_PCT,
                                  math.log(DECAY_TARGET) / SLOW_DECAY_PCT,
                                  HYENA_WIDTH, dtype=f32))
    window = jnp.exp(-t[:, None] * deltas[None, :]) + DECAY_SHIFT
    hf = hf * window[:, None, None, :]
    kern = jnp.concatenate([hf[:, :, 0],
                            jnp.zeros((1, HYENA_ORDER, HYENA_WIDTH), f32),
                            hf[:0:-1, :, 1]], axis=0)
    kern = kern / jnp.sum(jnp.abs(kern), axis=0, keepdims=True)
    return jnp.fft.rfft(kern, axis=0)


def long_conv(u, kf, skip):
    l = u.shape[1]
    uf = u.astype(jnp.float32)
    y = jnp.fft.irfft(jnp.fft.rfft(uf, n=2 * l, axis=1) * kf[None], n=2 * l, axis=1)[:, :l]
    return (y + uf * skip.astype(jnp.float32)).astype(u.dtype)


def attn_hyena_mixer(h, pos, p, j):
    b, l, _ = h.shape
    proj = h @ p['ab_w_in'][j]
    q, k, v, hy = jnp.split(proj, [ATTN_WIDTH, ATTN_WIDTH + KV_WIDTH, ATTN_WIDTH + 2 * KV_WIDTH], axis=-1)
    q = partial_rope(rms_norm(q.reshape(b, l, N_HEADS, HEAD_DIM), p['q_norm'][j]), pos)
    k = partial_rope(rms_norm(k.reshape(b, l, N_KV_HEADS, HEAD_DIM), p['k_norm'][j]), pos)
    v = v.reshape(b, l, N_KV_HEADS, HEAD_DIM)
    y_attn = window_attention(q, k, v, p['attn_sink'][j])
    hy = depthwise_conv(hy, p['hy_conv_w'][j], p['hy_conv_b'][j])
    gates = jnp.split(hy, HYENA_ORDER + 1, axis=-1)
    kf = hyena_filters(l, p['hy_f_w1'][j], p['hy_f_b1'][j], p['hy_f_freq1'][j],
                       p['hy_f_w2'][j], p['hy_f_b2'][j], p['hy_f_freq2'][j], p['hy_f_w3'][j])
    z = gates[-1]
    for o in range(HYENA_ORDER):
        z = gates[o] * long_conv(z, kf[:, o], p['hy_skip'][j, o])
    return jnp.concatenate([y_attn, z], axis=-1) @ p['ab_w_out'][j]


def conformer_conv(h, p, j):
    a = h @ p['cv_pw1_w'][j] + p['cv_pw1_b'][j]
    a = a[..., :CONV_INNER] * jax.nn.sigmoid(a[..., CONV_INNER:])
    a = depthwise_conv(a, p['cv_dw_w'][j], p['cv_dw_b'][j])
    a = jax.nn.silu(layer_norm(a, p['cv_ln_g'][j], p['cv_ln_b'][j]))
    return a @ p['cv_pw2_w'][j] + p['cv_pw2_b'][j]


def route(h, router_w, router_bias):
    t = h.shape[0]
    s = jax.nn.sigmoid(jnp.dot(h, router_w, preferred_element_type=jnp.float32))
    sb = (s + router_bias.astype(jnp.float32)).reshape(t, N_GROUPS, EXPERTS_PER_GROUP)
    gscore = jnp.sum(lax.top_k(sb, TOP_K)[0], axis=-1)
    g = jnp.argmax(gscore, axis=-1)
    sel = jnp.take_along_axis(sb, g[:, None, None], axis=1)[:, 0]
    _, local = lax.top_k(sel, TOP_K)
    eid = g[:, None] * EXPERTS_PER_GROUP + local
    w = jnp.take_along_axis(s, eid, axis=1)
    w = w / jnp.sum(w, axis=-1, keepdims=True)
    return eid, w


def moe_ffn(h, router_w, router_bias, w_gate, w_up, w_down):
    t, d = h.shape
    eid, w = route(h, router_w, router_bias)
    a = t * TOP_K
    e_flat = eid.reshape(a)
    w_flat = w.reshape(a).astype(h.dtype)
    tok = (jnp.arange(a) // TOP_K).astype(jnp.int32)
    order = jnp.argsort(e_flat)
    e_sorted = e_flat[order]
    counts = jnp.bincount(e_flat, length=N_EXPERTS)
    starts = jnp.cumsum(counts) - counts
    pcounts = (counts + MOE_BLOCK - 1) // MOE_BLOCK * MOE_BLOCK
    pends = jnp.cumsum(pcounts)
    pstarts = pends - pcounts
    dest = pstarts[e_sorted] + jnp.arange(a) - starts[e_sorted]
    nblk = -(-a // MOE_BLOCK) + N_EXPERTS
    n_slots = nblk * MOE_BLOCK
    slot_tok = jnp.zeros((n_slots,), jnp.int32).at[dest].set(tok[order])
    slot_w = jnp.zeros((n_slots,), h.dtype).at[dest].set(w_flat[order])
    blk_e = jnp.minimum(jnp.searchsorted(pends, jnp.arange(nblk) * MOE_BLOCK, side='right'), N_EXPERTS - 1)
    xb = h[slot_tok].reshape(nblk, MOE_BLOCK, d)

    def expert_block(args):
        xblk, e = args
        return (jax.nn.silu(xblk @ w_gate[e]) * (xblk @ w_up[e])) @ w_down[e]

    yb = lax.map(expert_block, (xb, blk_e))
    return jnp.zeros((t, d), h.dtype).at[slot_tok].add(yb.reshape(n_slots, d) * slot_w[:, None])


def trunk(x, c, p):
    b, l, d = x.shape
    pos = jnp.arange(l)
    c_act = jax.nn.silu(c)
    for i in range(DEPTH):
        mod = c_act @ p['ada_w'][i] + p['ada_b'][i]
        sh1, sc1, g1, sh2, sc2, g2 = jnp.split(mod, N_MOD, axis=-1)
        hmix = modulate(rms_norm(x, p['mix_norm'][i]), sh1, sc1)
        if i % 2 == 0:
            y = attn_hyena_mixer(hmix, pos, p, i // 2)
        else:
            y = conformer_conv(hmix, p, i // 2)
        x = x + g1[:, None, :] * y
        hffn = modulate(rms_norm(x, p['ffn_norm'][i]), sh2, sc2)
        y = moe_ffn(hffn.reshape(b * l, d), p['router_w'], p['router_bias'],
                    p['e_w_gate'][i], p['e_w_up'][i], p['e_w_down'][i])
        x = x + g2[:, None, :] * y.reshape(b, l, d)
    return x


def setup_inputs(seed: int = 0) -> dict:
    key = jax.random.key(seed)
    ks = iter(jax.random.split(key, 48))

    def nrm(shape, scale):
        return jax.random.normal(next(ks), shape, jnp.float32) * scale

    def gain(shape):
        return 1.0 + nrm(shape, 0.02)

    D = D_MODEL
    HC = (HYENA_ORDER + 1) * HYENA_WIDTH
    return {
        'x_prompt': nrm((BATCH, SEQ, D), 1.0),
        'x_sample': nrm((DEC_BATCH, DEC_SEQ, D), 1.0),
        'c_prompt': nrm((BATCH, D), 1.0),
        'c_sample': nrm((DEC_BATCH, D), 1.0),
        'mix_norm': gain((DEPTH, D)),
        'ffn_norm': gain((DEPTH, D)),
        'ada_w': nrm((DEPTH, D, N_MOD * D), 0.5 * D ** -0.5),
        'ada_b': nrm((DEPTH, N_MOD * D), 0.02),
        'ab_w_in': nrm((N_EVEN, D, IN_WIDTH), D ** -0.5),
        'ab_w_out': nrm((N_EVEN, MIX_WIDTH, D), MIX_WIDTH ** -0.5),
        'q_norm': gain((N_EVEN, HEAD_DIM)),
        'k_norm': gain((N_EVEN, HEAD_DIM)),
        'attn_sink': nrm((N_EVEN, N_HEADS), 0.5),
        'hy_conv_w': nrm((N_EVEN, SHORT_CONV, HC), SHORT_CONV ** -0.5),
        'hy_conv_b': nrm((N_EVEN, HC), 0.02),
        'hy_f_w1': nrm((N_EVEN, POS_EMB, FILTER_HIDDEN), POS_EMB ** -0.5),
        'hy_f_b1': nrm((N_EVEN, FILTER_HIDDEN), 0.1),
        'hy_f_freq1': 1.0 + nrm((N_EVEN, FILTER_HIDDEN), 0.1),
        'hy_f_w2': nrm((N_EVEN, FILTER_HIDDEN, FILTER_HIDDEN), FILTER_HIDDEN ** -0.5),
        'hy_f_b2': nrm((N_EVEN, FILTER_HIDDEN), 0.1),
        'hy_f_freq2': 1.0 + nrm((N_EVEN, FILTER_HIDDEN), 0.1),
        'hy_f_w3': nrm((N_EVEN, FILTER_HIDDEN, HYENA_ORDER * 2 * HYENA_WIDTH), FILTER_HIDDEN ** -0.5),
        'hy_skip': nrm((N_EVEN, HYENA_ORDER, HYENA_WIDTH), 0.5),
        'cv_pw1_w': nrm((N_ODD, D, 2 * CONV_INNER), D ** -0.5),
        'cv_pw1_b': nrm((N_ODD, 2 * CONV_INNER), 0.02),
        'cv_dw_w': nrm((N_ODD, CONV_KERNEL, CONV_INNER), CONV_KERNEL ** -0.5),
        'cv_dw_b': nrm((N_ODD, CONV_INNER), 0.02),
        'cv_ln_g': gain((N_ODD, CONV_INNER)),
        'cv_ln_b': nrm((N_ODD, CONV_INNER), 0.02),
        'cv_pw2_w': nrm((N_ODD, CONV_INNER, D), CONV_INNER ** -0.5),
        'cv_pw2_b': nrm((N_ODD, D), 0.02),
        'router_w': nrm((D, N_EXPERTS), D ** -0.5),
        'router_bias': nrm((N_EXPERTS,), 0.01),
        'e_w_gate': nrm((DEPTH, N_EXPERTS, D, D_EXPERT), D ** -0.5),
        'e_w_up': nrm((DEPTH, N_EXPERTS, D, D_EXPERT), D ** -0.5),
        'e_w_down': nrm((DEPTH, N_EXPERTS, D_EXPERT, D), D_EXPERT ** -0.5),
    }


def reference(x_prompt, x_sample, c_prompt, c_sample, mix_norm, ffn_norm, ada_w, ada_b,
              ab_w_in, ab_w_out, q_norm, k_norm, attn_sink, hy_conv_w, hy_conv_b,
              hy_f_w1, hy_f_b1, hy_f_freq1, hy_f_w2, hy_f_b2, hy_f_freq2, hy_f_w3, hy_skip,
              cv_pw1_w, cv_pw1_b, cv_dw_w, cv_dw_b, cv_ln_g, cv_ln_b, cv_pw2_w, cv_pw2_b,
              router_w, router_bias, e_w_gate, e_w_up, e_w_down):
    p = dict(mix_norm=mix_norm, ffn_norm=ffn_norm, ada_w=ada_w, ada_b=ada_b,
             ab_w_in=ab_w_in, ab_w_out=ab_w_out, q_norm=q_norm, k_norm=k_norm,
             attn_sink=attn_sink, hy_conv_w=hy_conv_w, hy_conv_b=hy_conv_b,
             hy_f_w1=hy_f_w1, hy_f_b1=hy_f_b1, hy_f_freq1=hy_f_freq1,
             hy_f_w2=hy_f_w2, hy_f_b2=hy_f_b2, hy_f_freq2=hy_f_freq2, hy_f_w3=hy_f_w3,
             hy_skip=hy_skip, cv_pw1_w=cv_pw1_w, cv_pw1_b=cv_pw1_b, cv_dw_w=cv_dw_w,
             cv_dw_b=cv_dw_b, cv_ln_g=cv_ln_g, cv_ln_b=cv_ln_b, cv_pw2_w=cv_pw2_w,
             cv_pw2_b=cv_pw2_b, router_w=router_w, router_bias=router_bias,
             e_w_gate=e_w_gate, e_w_up=e_w_up, e_w_down=e_w_down)
    y_prompt = trunk(x_prompt, c_prompt, p)
    y_sample = trunk(x_sample, c_sample, p)
    return (y_prompt, y_sample)
```

```python
import functools
import math

import jax
import jax.numpy as jnp
from jax import lax
from jax.experimental import pallas as pl
from jax.experimental.pallas import tpu as pltpu

F32 = jnp.float32
BF16 = jnp.bfloat16
I32 = jnp.int32
U32 = jnp.uint32

N_HEADS = 8
N_KV_HEADS = 2
HEAD_DIM = 64
WINDOW = 128
ATTN_BLOCK = 128
ROPE_DIM = HEAD_DIM // 4
ROPE_THETA = 500000.0
NEG_INF = -1e30
POS_FREQS = 16
FAST_DECAY_PCT = 0.3
SLOW_DECAY_PCT = 1.5
DECAY_TARGET = 0.01
DECAY_SHIFT = 0.05
N_GROUPS = 8
EPS = 1e-6

LANES = 128
ROW_BLOCK = 512
EXPERT_BLOCK = 256
MOVE_BLOCK = 256
HYENA_CBLOCK = 256
HYENA_FBLOCK = 512
KV_DUP_WIDTH =2 * N_KV_HEADS * HEAD_DIM
VMEM_LIMIT = 56 * 1024 * 1024

_HI = lax.Precision.HIGHEST


def _dot(a, b, precision=None):
    return jnp.dot(a, b, preferred_element_type=F32, precision=precision)


def _dot_nt(a, b):
    return lax.dot_general(a, b, (((1,), (1,)), ((), ())), preferred_element_type=F32)


def _sigmoid(x):
    return 1.0 / (1.0 + jnp.exp(-x))


def _rms_mod(x, gain, shift, scale):
    ms = jnp.mean(x * x, axis=-1, keepdims=True)
    y = x * lax.rsqrt(ms + EPS) * gain
    return y * (1.0 + scale) + shift


def _cparams(sem, vmem=VMEM_LIMIT):
    return pltpu.CompilerParams(dimension_semantics=sem, vmem_limit_bytes=vmem)


def _ada_kernel(c_ref, w_ref, b_ref, o_ref):
    c = c_ref[...]
    ca = c * _sigmoid(c)
    o_ref[0] = _dot(ca, w_ref[0], _HI) + b_ref[0]


def _ada(c, ada_w, ada_b):
    depth, d, n = ada_w.shape
    b = c.shape[0]
    tn = n // 4
    return pl.pallas_call(
        _ada_kernel,
        grid=(depth, n // tn),
        in_specs=[
            pl.BlockSpec((b, d), lambda i, j: (0, 0)),
            pl.BlockSpec((1, d, tn), lambda i, j: (i, 0, j)),
            pl.BlockSpec((1, 1, tn), lambda i, j: (i, 0, j)),
        ],
        out_specs=pl.BlockSpec((1, b, tn), lambda i, j: (i, 0, j)),
        out_shape=jax.ShapeDtypeStruct((depth, b, n), F32),
        compiler_params=_cparams(("arbitrary", "arbitrary")),
        name="ada_mod",
    )(c, ada_w, ada_b.reshape(depth, 1, n))


def _headnorm_rope(t, gain, bd, c, sa, sb):
    ms = _dot((t * t).astype(BF16), bd)
    tn = t * lax.rsqrt(ms + EPS) * gain
    n = t.shape[1]
    half = ROPE_DIM // 2
    return tn * c + pltpu.roll(tn, half, 1) * sa + pltpu.roll(tn, n - half, 1) * sb


def _inproj_kernel(x_ref, mod_ref, g_ref, w_ref, gq_ref, gk_ref, bd_ref, c_ref, sa_ref, sb_ref,
                   q_ref, k_ref, v_ref, hy_ref):
    d = x_ref.shape[1]
    aw = N_HEADS * HEAD_DIM
    kw = KV_DUP_WIDTH
    m = mod_ref[0]
    h = _rms_mod(x_ref[...], g_ref[...], m[:, 0:d], m[:, d:2 * d])
    proj = _dot(h.astype(BF16), w_ref[...])
    q = proj[:, :aw]
    k = proj[:, aw:aw + kw]
    c, sa, sb = c_ref[...], sa_ref[...], sb_ref[...]
    bd = bd_ref[...]
    qr = _headnorm_rope(q, gq_ref[...], bd, c, sa, sb)
    kr = _headnorm_rope(k, gk_ref[...], bd[:kw, :kw], c[:, :kw], sa[:, :kw], sb[:, :kw])
    q_ref[...] = (qr * (HEAD_DIM ** -0.5)).astype(BF16)
    k_ref[...] = kr.astype(BF16)
    v_ref[...] = proj[:, aw + kw:aw + 2 * kw].astype(BF16)
    hy_ref[...] = proj[:, aw + 2 * kw:].astype(BF16)


def _inproj(x, mod3, gain, w, gq, gk, bd, rc, rsa, rsb, seq):
    t, d = x.shape
    bm = ROW_BLOCK
    nlb = seq // bm
    aw = N_HEADS * HEAD_DIM
    kw = KV_DUP_WIDTH
    hw = w.shape[1] - aw - 2 * kw
    row = lambda i: (i, 0)
    full = lambda i: (0, 0)
    tab = lambda i: (i % nlb, 0)
    return pl.pallas_call(
        _inproj_kernel,
        grid=(t // bm,),
        in_specs=[
            pl.BlockSpec((bm, d), row),
            pl.BlockSpec((1, 1, mod3.shape[2]), lambda i: (i // nlb, 0, 0)),
            pl.BlockSpec((1, d), full),
            pl.BlockSpec(w.shape, full),
            pl.BlockSpec((1, aw), full),
            pl.BlockSpec((1, kw), full),
            pl.BlockSpec((aw, aw), full),
            pl.BlockSpec((bm, aw), tab),
            pl.BlockSpec((bm, aw), tab),
            pl.BlockSpec((bm, aw), tab),
        ],
        out_specs=[
            pl.BlockSpec((bm, aw), row),
            pl.BlockSpec((bm, kw), row),
            pl.BlockSpec((bm, kw), row),
            pl.BlockSpec((bm, hw), row),
        ],
        out_shape=[
            jax.ShapeDtypeStruct((t, aw), BF16),
            jax.ShapeDtypeStruct((t, kw), BF16),
            jax.ShapeDtypeStruct((t, kw), BF16),
            jax.ShapeDtypeStruct((t, hw), BF16),
        ],
        compiler_params=_cparams(("arbitrary",)),
        name="inproj",
    )(x, mod3, gain, w, gq, gk, bd, rc, rsa, rsb)


def _attn_kernel(sink_ref, q_ref, k_ref, v_ref, o_ref):
    n = pl.program_id(1)
    seq = k_ref.shape[0]
    blk = ATTN_BLOCK
    span = 3 * blk
    ws = pl.multiple_of(jnp.clip((n - 1) * blk, 0, seq - span), blk)
    kwin = k_ref[pl.ds(ws, span), :]
    vwin = v_ref[pl.ds(ws, span), :]
    pair = 2 * HEAD_DIM
    low = lax.broadcasted_iota(I32, (span, pair), 1) < HEAD_DIM
    zero = jnp.zeros((span, pair), kwin.dtype)

    def halves(t, g):
        tg = t[:, g * pair:(g + 1) * pair]
        return jnp.where(low, tg, zero), jnp.where(low, zero, tg)

    ks = tuple(halves(kwin, g) for g in range(N_KV_HEADS))
    vs = tuple(halves(vwin, g) for g in range(N_KV_HEADS))
    qpos = n * blk + lax.broadcasted_iota(I32, (blk, span), 0)
    kpos = ws + lax.broadcasted_iota(I32, (blk, span), 1)
    valid = jnp.abs(kpos - qpos) <= WINDOW
    q_group = N_HEADS // N_KV_HEADS
    for p in range(N_HEADS // 2):
        qp = q_ref[:, p * 2 * HEAD_DIM:(p + 1) * 2 * HEAD_DIM]
        acc = None
        for r in range(2):
            hd = 2 * p + r
            g = hd // q_group
            s = _dot_nt(qp, ks[g][r])
            s = jnp.where(valid, s, NEG_INF)
            sk = sink_ref[hd]
            mx = jnp.maximum(jnp.max(s, axis=-1, keepdims=True), sk)
            e = jnp.exp(s - mx)
            den = jnp.sum(e, axis=-1, keepdims=True) + jnp.exp(sk - mx)
            pr = (e / den).astype(BF16)
            o = _dot(pr, vs[g][r])
            acc = o if acc is None else acc + o
        o_ref[:, p * 2 * HEAD_DIM:(p + 1) * 2 * HEAD_DIM] = acc.astype(BF16)


def _attention(q, k, v, sink, batch, seq):
    t, aw = q.shape
    kw = k.shape[1]
    nb = seq // ATTN_BLOCK
    return pl.pallas_call(
        _attn_kernel,
        grid=(batch, nb),
        in_specs=[
            pl.BlockSpec(memory_space=pltpu.SMEM),
            pl.BlockSpec((ATTN_BLOCK, aw), lambda b, n: (b * nb + n, 0)),
            pl.BlockSpec((seq, kw), lambda b, n: (b, 0)),
            pl.BlockSpec((seq, kw), lambda b, n: (b, 0)),
        ],
        out_specs=pl.BlockSpec((ATTN_BLOCK, aw), lambda b, n: (b * nb + n, 0)),
        out_shape=jax.ShapeDtypeStruct((t, aw), BF16),
        compiler_params=_cparams(("arbitrary", "arbitrary")),
        name="window_attn",
    )(sink, q, k, v)


def _split_dot(mat, v):
    vh = v.astype(BF16)
    vl = (v - vh.astype(F32)).astype(BF16)
    return _dot(mat, vh) + _dot(mat, vl)


def _filter_kernel(z_ref, w1_ref, b1_ref, f1_ref, w2_ref, b2_ref, f2_ref, w3_ref, win_ref,
                   cm_ref, sm_ref, kc_ref, ks_ref, kn_ref):
    h1 = jnp.sin(f1_ref[...] * (_dot(z_ref[...], w1_ref[...], _HI) + b1_ref[...]))
    h2 = jnp.sin(f2_ref[...] * (_dot(h1, w2_ref[...], _HI) + b2_ref[...]))
    win = win_ref[...]
    hf = _dot(h2, w3_ref[0, 0], _HI) * win
    hg = _dot(h2, w3_ref[0, 1], _HI) * win
    row = lax.broadcasted_iota(I32, hf.shape, 0)
    hg = jnp.where(row == 0, 0.0, hg)
    nrm = jnp.sum(jnp.abs(hf), axis=0, keepdims=True) + jnp.sum(jnp.abs(hg), axis=0, keepdims=True)
    hf = hf / nrm
    hg = hg / nrm
    even = hf + hg
    odd = hf - hg
    kc_ref[0] = _split_dot(cm_ref[...], even)
    ks_ref[0] = _split_dot(sm_ref[...], odd)
    alt = (1 - 2 * (row & 1)).astype(F32)
    kn_ref[0] = jnp.sum(alt * even, axis=0, keepdims=True)


def _single(shape, index_map):
    return pl.BlockSpec(shape, index_map, pipeline_mode=pl.Buffered(1))


def _hyena_spectra(zf, w1, b1, f1, w2, b2, f2, w3, win, cm, sm):
    seq = zf.shape[0]
    order, _, hidden, width = w3.shape
    cb = HYENA_CBLOCK
    full2 = lambda o, c: (0, 0)
    return pl.pallas_call(
        _filter_kernel,
        grid=(order, width // cb),
        in_specs=[
            pl.BlockSpec(zf.shape, full2),
            pl.BlockSpec(w1.shape, full2),
            pl.BlockSpec(b1.shape, full2),
            pl.BlockSpec(f1.shape, full2),
            pl.BlockSpec(w2.shape, full2),
            pl.BlockSpec(b2.shape, full2),
            pl.BlockSpec(f2.shape, full2),
            pl.BlockSpec((1, 2, hidden, cb), lambda o, c: (o, 0, 0, c)),
            pl.BlockSpec((seq, cb), lambda o, c: (0, c)),
            _single((seq, seq), full2),
            _single((seq, seq), full2),
        ],
        out_specs=[
            pl.BlockSpec((1, seq, cb), lambda o, c: (o, 0, c)),
            pl.BlockSpec((1, seq, cb), lambda o, c: (o, 0, c)),
            pl.BlockSpec((1, 1, cb), lambda o, c: (o, 0, c)),
        ],
        out_shape=[
            jax.ShapeDtypeStruct((order, seq, width), F32),
            jax.ShapeDtypeStruct((order, seq, width), F32),
            jax.ShapeDtypeStruct((order, 1, width), F32),
        ],
        compiler_params=_cparams(("arbitrary", "arbitrary")),
        name="hyena_spectra",
    )(zf, w1, b1, f1, w2, b2, f2, w3, win, cm, sm)


def _hyena_kernel(x1_ref, x2_ref, v_ref, cw1_ref, cw2_ref, cwv_ref, cb1_ref, cb2_ref, cbv_ref,
                  cm_ref, sm_ref, kc_ref, ks_ref, kn_ref, skip_ref, o_ref, yc_ref, ys_ref):
    seq = x1_ref.shape[0]
    row = lax.broadcasted_iota(I32, (seq, 1), 0)
    alt = (1 - 2 * (row & 1)).astype(F32)
    inv_n = 1.0 / (2 * seq)

    def short_conv(ref, w_ref, b_ref):
        x = ref[...].astype(F32)
        w = w_ref[...]
        prev = jnp.where(row == 0, 0.0, pltpu.roll(x, 1, 0))
        nxt = jnp.where(row == seq - 1, 0.0, pltpu.roll(x, seq - 1, 0))
        return prev * w[0:1] + x * w[1:2] + nxt * w[2:3] + b_ref[...]

    fblk = min(seq, HYENA_FBLOCK)

    def long_conv(u, o):
        ub = u.astype(BF16)
        zn = jnp.sum(alt * u, axis=0, keepdims=True)
        for f0 in range(0, seq, fblk):
            fs = slice(f0, f0 + fblk)
            zc = _dot(cm_ref[fs, :], ub)
            zs = _dot(sm_ref[fs, :], ub)
            kc = kc_ref[o, fs, :]
            ks = ks_ref[o, fs, :]
            yc = zc * kc - zs * ks
            if f0 == 0:
                yc = jnp.where(row[fs] == 0, 0.5 * yc, yc)
            yc_ref[fs, :] = yc.astype(BF16)
            ys_ref[fs, :] = (zc * ks + zs * kc).astype(BF16)
        y = _dot(cm_ref[...], yc_ref[...]) + _dot(sm_ref[...], ys_ref[...])
        y = (2.0 * y + alt * (zn * kn_ref[o])) * inv_n
        return y + u * skip_ref[o:o + 1]

    z = short_conv(v_ref, cwv_ref, cbv_ref)
    z = short_conv(x1_ref, cw1_ref, cb1_ref) * long_conv(z, 0)
    z = short_conv(x2_ref, cw2_ref, cb2_ref) * long_conv(z, 1)
    o_ref[...] = z.astype(BF16)


def _hyena(hy, conv_w, conv_b, cm, sm, kc, ks, kn, skip, batch, seq):
    t = hy.shape[0]
    width = hy.shape[1] // 3
    cb = HYENA_CBLOCK
    ncb = width // cb
    order = kc.shape[0]
    taps = conv_w.shape[0]
    full2 = lambda c, b: (0, 0)

    def col(k):
        return lambda c, b: (b, k * ncb + c)

    def wcol(k):
        return lambda c, b: (0, k * ncb + c)

    return pl.pallas_call(
        _hyena_kernel,
        grid=(ncb, batch),
        in_specs=[
            pl.BlockSpec((seq, cb), col(0)),
            pl.BlockSpec((seq, cb), col(1)),
            pl.BlockSpec((seq, cb), col(2)),
            pl.BlockSpec((taps, cb), wcol(0)),
            pl.BlockSpec((taps, cb), wcol(1)),
            pl.BlockSpec((taps, cb), wcol(2)),
            pl.BlockSpec((1, cb), wcol(0)),
            pl.BlockSpec((1, cb), wcol(1)),
            pl.BlockSpec((1, cb), wcol(2)),
            _single((seq, seq), full2),
            _single((seq, seq), full2),
            _single((order, seq, cb), lambda c, b: (0, 0, c)),
            _single((order, seq, cb), lambda c, b: (0, 0, c)),
            pl.BlockSpec((order, 1, cb), lambda c, b: (0, 0, c)),
            pl.BlockSpec((order, cb), lambda c, b: (0, c)),
        ],
        out_specs=pl.BlockSpec((seq, cb), lambda c, b: (b, c)),
        out_shape=jax.ShapeDtypeStruct((t, width), BF16),
        scratch_shapes=[pltpu.VMEM((seq, cb), BF16), pltpu.VMEM((seq, cb), BF16)],
        compiler_params=_cparams(("arbitrary", "arbitrary")),
        name="hyena",
    )(hy, hy, hy, conv_w, conv_w, conv_w, conv_b, conv_b, conv_b, cm, sm, kc, ks, kn, skip)


def _ffn_prologue(xn, m, fg_ref, rwh_ref, rwl_ref, rb_ref, tri_ref,
                  hp_ref, eid_ref, wt_ref, rank_ref, cnt_ref, carry_ref):
    bm, d = xn.shape
    n_exp = rb_ref.shape[0]
    epg = n_exp // N_GROUPS
    hf = _rms_mod(xn, fg_ref[...], m[:, 3 * d:4 * d], m[:, 4 * d:5 * d])
    hb = hf.astype(BF16)
    hbf = hb.astype(F32)
    lo = pltpu.bitcast(hbf[:, :d // 2], U32) >> 16
    hi = pltpu.bitcast(hbf[:, d // 2:], U32) & jnp.uint32(0xFFFF0000)
    hp_ref[...] = lo | hi

    hl = (hf - hbf).astype(BF16)
    lg = _dot(hb, rwh_ref[...]) + _dot(hl, rwh_ref[...]) + _dot(hb, rwl_ref[...])
    lt = lg.T[:n_exp]
    s = _sigmoid(lt)
    sb = s + rb_ref[:, 0:1]
    s3 = s.reshape(N_GROUPS, epg, bm)
    sb3 = sb.reshape(N_GROUPS, epg, bm)
    ji = lax.broadcasted_iota(I32, sb3.shape, 1)
    gi = lax.broadcasted_iota(I32, sb3.shape, 0)
    ninf = -jnp.inf
    m1 = jnp.max(sb3, axis=1, keepdims=True)
    i1 = jnp.min(jnp.where(sb3 == m1, ji, epg), axis=1, keepdims=True)
    m2 = jnp.max(jnp.where(ji == i1, ninf, sb3), axis=1, keepdims=True)
    gs = m1 + m2
    gm = jnp.max(gs, axis=0, keepdims=True)
    gidx = lax.broadcasted_iota(I32, gs.shape, 0)
    gbest = jnp.min(jnp.where(gs == gm, gidx, N_GROUPS), axis=0, keepdims=True)
    pick = gi == gbest
    sel = jnp.max(jnp.where(pick, sb3, ninf), axis=0)
    ssel = jnp.max(jnp.where(pick, s3, ninf), axis=0)
    jj = lax.broadcasted_iota(I32, sel.shape, 0)
    a1 = jnp.max(sel, axis=0, keepdims=True)
    j1 = jnp.min(jnp.where(sel == a1, jj, epg), axis=0, keepdims=True)
    sel2 = jnp.where(jj == j1, ninf, sel)
    a2 = jnp.max(sel2, axis=0, keepdims=True)
    j2 = jnp.min(jnp.where(sel2 == a2, jj, epg), axis=0, keepdims=True)
    w1 = jnp.sum(jnp.where(jj == j1, ssel, 0.0), axis=0, keepdims=True)
    w2 = jnp.sum(jnp.where(jj == j2, ssel, 0.0), axis=0, keepdims=True)
    den = w1 + w2
    gb = gbest[0]
    e1 = gb * epg + j1
    e2 = gb * epg + j2
    eid_ref[0:1, :] = e1
    eid_ref[1:2, :] = e2
    wt_ref[0:1, :] = w1 / den
    wt_ref[1:2, :] = w2 / den

    ei = lax.broadcasted_iota(I32, (n_exp, bm), 0)
    oh = jnp.concatenate([jnp.where(ei == e1, 1.0, 0.0), jnp.where(ei == e2, 1.0, 0.0)], axis=1)
    pre = _dot(oh.astype(BF16), tri_ref[...])
    carry = carry_ref[...]
    rk = jnp.sum(oh * (pre + carry[:, 0:1]), axis=0, keepdims=True)
    rank_ref[0:1, :] = rk[:, :bm].astype(I32)
    rank_ref[1:2, :] = rk[:, bm:].astype(I32)
    carry = carry + jnp.sum(oh, axis=1, keepdims=True)
    carry_ref[...] = carry
    cnt_ref[...] = carry


def _outproj_kernel(ya_ref, z_ref, x_ref, mod_ref, w_ref, fg_ref, rwh_ref, rwl_ref, rb_ref, tri_ref,
                    xo_ref, hp_ref, eid_ref, wt_ref, rank_ref, cnt_ref, carry_ref):
    @pl.when(pl.program_id(0) == 0)
    def _():
        carry_ref[...] = jnp.zeros_like(carry_ref)

    d = x_ref.shape[1]
    aw = ya_ref.shape[1]
    m = mod_ref[0]
    y = _dot(ya_ref[...], w_ref[:aw, :]) + _dot(z_ref[...], w_ref[aw:, :])
    xn = x_ref[...] + m[:, 2 * d:3 * d] * y
    xo_ref[...] = xn
    _ffn_prologue(xn, m, fg_ref, rwh_ref, rwl_ref, rb_ref, tri_ref,
                  hp_ref, eid_ref, wt_ref, rank_ref, cnt_ref, carry_ref)


def _router_specs(d, n_exp, bm, full, tok):
    in_specs = [
        pl.BlockSpec((1, d), full),
        pl.BlockSpec((d, LANES), full),
        pl.BlockSpec((d, LANES), full),
        pl.BlockSpec((n_exp, LANES), full),
        pl.BlockSpec((2 * bm, 2 * bm), full),
    ]
    out_specs = [
        pl.BlockSpec((bm, d // 2), tok[0]),
        pl.BlockSpec((2, bm), tok[1]),
        pl.BlockSpec((2, bm), tok[1]),
        pl.BlockSpec((2, bm), tok[1]),
        pl.BlockSpec((n_exp, LANES), full),
    ]
    return in_specs, out_specs


def _router_out_shapes(t, d, n_exp):
    return [
        jax.ShapeDtypeStruct((t, d // 2), U32),
        jax.ShapeDtypeStruct((2, t), I32),
        jax.ShapeDtypeStruct((2, t), F32),
        jax.ShapeDtypeStruct((2, t), I32),
        jax.ShapeDtypeStruct((n_exp, LANES), F32),
    ]


def _outproj(ya, z, x, mod3, w, fgain, rwh, rwl, rb, tri, seq):
    t, d = x.shape
    bm = ROW_BLOCK
    nlb = seq // bm
    n_exp = rb.shape[0]
    row = lambda i: (i, 0)
    full = lambda i: (0, 0)
    r_in, r_out = _router_specs(d, n_exp, bm, full, (row, lambda i: (0, i)))
    return pl.pallas_call(
        _outproj_kernel,
        grid=(t // bm,),
        in_specs=[
            pl.BlockSpec((bm, ya.shape[1]), row),
            pl.BlockSpec((bm, z.shape[1]), row),
            pl.BlockSpec((bm, d), row),
            pl.BlockSpec((1, 1, mod3.shape[2]), lambda i: (i // nlb, 0, 0)),
            pl.BlockSpec(w.shape, full),
        ] + r_in,
        out_specs=[pl.BlockSpec((bm, d), row)] + r_out,
        out_shape=[jax.ShapeDtypeStruct((t, d), F32)] + _router_out_shapes(t, d, n_exp),
        scratch_shapes=[pltpu.VMEM((n_exp, LANES), F32)],
        compiler_params=_cparams(("arbitrary",)),
        name="outproj_router",
    )(ya, z, x, mod3, w, fgain, rwh, rwl, rb, tri)


def _dispatch_kernel(dest_hbm, h_ref, xb_in, xb_hbm, dest_smem, isem, sem):
    del xb_in
    i = pl.program_id(0)
    bm = h_ref.shape[0]
    cp = pltpu.make_async_copy(dest_hbm.at[i], dest_smem, isem)
    cp.start()
    cp.wait()

    def row_copy(t, dst):
        return pltpu.make_async_copy(h_ref.at[pl.ds(t, 1)], xb_hbm.at[pl.ds(dst, 1)], sem)

    def issue(t, carry):
        row_copy(t, dest_smem[0, t]).start()
        row_copy(t, dest_smem[1, t]).start()
        return carry

    lax.fori_loop(0, bm, issue, 0, unroll=8)

    def drain(t, carry):
        row_copy(0, 0).wait()
        row_copy(0, 0).wait()
        return carry

    lax.fori_loop(0, bm, drain, 0, unroll=8)


def _dispatch(dest3, hp, n_slots):
    t, dw = hp.shape
    bm = MOVE_BLOCK
    zeros = jnp.zeros((n_slots, dw), U32)
    return pl.pallas_call(
        _dispatch_kernel,
        grid=(t // bm,),
        in_specs=[
            pl.BlockSpec(memory_space=pl.ANY),
            pl.BlockSpec((bm, dw), lambda i: (i, 0)),
            pl.BlockSpec(memory_space=pl.ANY),
        ],
        out_specs=pl.BlockSpec(memory_space=pl.ANY),
        out_shape=jax.ShapeDtypeStruct((n_slots, dw), U32),
        scratch_shapes=[
            pltpu.SMEM((2, bm), I32),
            pltpu.SemaphoreType.DMA,
            pltpu.SemaphoreType.DMA,
        ],
        input_output_aliases={2: 0},
        compiler_params=_cparams(("arbitrary",)),
        name="moe_dispatch",
    )(dest3, hp, zeros)


def _expert_kernel(be_ref, nu_ref, x_ref, wg_ref, wu_ref, wd_ref, o_ref):
    j = pl.program_id(0)

    @pl.when(j < nu_ref[0])
    def _():
        xp = x_ref[...]
        half = xp.shape[1]
        a = pltpu.bitcast(xp << 16, F32).astype(BF16)
        b = pltpu.bitcast(xp & jnp.uint32(0xFFFF0000), F32).astype(BF16)
        g = _dot(a, wg_ref[0, :half, :]) + _dot(b, wg_ref[0, half:, :])
        u = _dot(a, wu_ref[0, :half, :]) + _dot(b, wu_ref[0, half:, :])
        hm = (g * _sigmoid(g) * u).astype(BF16)
        o_ref[...] = _dot(hm, wd_ref[0])

    @pl.when(j >= nu_ref[0])
    def _():
        o_ref[...] = jnp.zeros_like(o_ref)


def _experts(blk_e, n_used, xb, wg, wu, wd):
    n_slots, dw = xb.shape
    _, d, de = wg.shape
    bm = EXPERT_BLOCK
    grid_spec = pltpu.PrefetchScalarGridSpec(
        num_scalar_prefetch=2,
        grid=(n_slots // bm,),
        in_specs=[
            pl.BlockSpec((bm, dw), lambda j, be, nu: (j, 0)),
            pl.BlockSpec((1, d, de), lambda j, be, nu: (be[j], 0, 0)),
            pl.BlockSpec((1, d, de), lambda j, be, nu: (be[j], 0, 0)),
            pl.BlockSpec((1, de, d), lambda j, be, nu: (be[j], 0, 0)),
        ],
        out_specs=pl.BlockSpec((bm, d), lambda j, be, nu: (j, 0)),
    )
    return pl.pallas_call(
        _expert_kernel,
        grid_spec=grid_spec,
        out_shape=jax.ShapeDtypeStruct((n_slots, d), F32),
        compiler_params=_cparams(("arbitrary",)),
        name="moe_experts",
    )(blk_e, n_used, xb, wg, wu, wd)


def _combine_kernel(dest_hbm, yb_hbm, x_ref, wt_ref, mod_ref, o_ref, dest_smem, buf, isem, sem):
    i = pl.program_id(0)
    bm, d = x_ref.shape
    cp = pltpu.make_async_copy(dest_hbm.at[i], dest_smem, isem)
    cp.start()
    cp.wait()

    def row_copy(k, t, src):
        return pltpu.make_async_copy(yb_hbm.at[pl.ds(src, 1)], buf.at[k, pl.ds(t, 1)], sem)

    def issue(t, carry):
        row_copy(0, t, dest_smem[0, t]).start()
        row_copy(1, t, dest_smem[1, t]).start()
        return carry

    lax.fori_loop(0, bm, issue, 0, unroll=8)

    def drain(t, carry):
        row_copy(0, 0, 0).wait()
        row_copy(1, 0, 0).wait()
        return carry

    lax.fori_loop(0, bm, drain, 0, unroll=8)

    eye = lax.broadcasted_iota(I32, (bm, bm), 0) == lax.broadcasted_iota(I32, (bm, bm), 1)
    w = wt_ref[...]
    w0 = jnp.sum(jnp.where(eye, w[0:1, :], 0.0), axis=1, keepdims=True)
    w1 = jnp.sum(jnp.where(eye, w[1:2, :], 0.0), axis=1, keepdims=True)
    g2 = mod_ref[0][:, 5 * d:6 * d]
    o_ref[...] = x_ref[...] + g2 * (w0 * buf[0] + w1 * buf[1])


def _combine(dest3, yb, x, wts, mod3, seq):
    t, d = x.shape
    bm = MOVE_BLOCK
    nlb = seq // bm
    return pl.pallas_call(
        _combine_kernel,
        grid=(t // bm,),
        in_specs=[
            pl.BlockSpec(memory_space=pl.ANY),
            pl.BlockSpec(memory_space=pl.ANY),
            pl.BlockSpec((bm, d), lambda i: (i, 0)),
            pl.BlockSpec((2, bm), lambda i: (0, i)),
            pl.BlockSpec((1, 1, mod3.shape[2]), lambda i: (i // nlb, 0, 0)),
        ],
        out_specs=pl.BlockSpec((bm, d), lambda i: (i, 0)),
        out_shape=jax.ShapeDtypeStruct((t, d), F32),
        scratch_shapes=[
            pltpu.SMEM((2, bm), I32),
            pltpu.VMEM((2, bm, d), F32),
            pltpu.SemaphoreType.DMA,
            pltpu.SemaphoreType.DMA,
        ],
        compiler_params=_cparams(("arbitrary",)),
        name="moe_combine",
    )(dest3, yb, x, wts, mod3)


def _pw1_kernel(x_ref, mod_ref, g_ref, w_ref, b_ref, o_ref):
    d = x_ref.shape[1]
    ci = o_ref.shape[1]
    m = mod_ref[0]
    h = _rms_mod(x_ref[...], g_ref[...], m[:, 0:d], m[:, d:2 * d])
    a = _dot(h.astype(BF16), w_ref[...]) + b_ref[...]
    o_ref[...] = (a[:, :ci] * _sigmoid(a[:, ci:])).astype(BF16)


def _pw1(x, mod3, gain, w, b, seq):
    t, d = x.shape
    bm = ROW_BLOCK
    nlb = seq // bm
    ci = w.shape[1] // 2
    row = lambda i: (i, 0)
    full = lambda i: (0, 0)
    return pl.pallas_call(
        _pw1_kernel,
        grid=(t // bm,),
        in_specs=[
            pl.BlockSpec((bm, d), row),
            pl.BlockSpec((1, 1, mod3.shape[2]), lambda i: (i // nlb, 0, 0)),
            pl.BlockSpec((1, d), full),
            pl.BlockSpec(w.shape, full),
            pl.BlockSpec((1, w.shape[1]), full),
        ],
        out_specs=pl.BlockSpec((bm, ci), row),
        out_shape=jax.ShapeDtypeStruct((t, ci), BF16),
        compiler_params=_cparams(("arbitrary",)),
        name="conformer_pw1",
    )(x, mod3, gain, w, b)


CONV_HALO = 16


def _conv_kernel(a_ref, x_ref, mod_ref, dw_ref, db_ref, lg_ref, lb_ref, w_ref, pb_ref,
                 fg_ref, rwh_ref, rwl_ref, rb_ref, tri_ref,
                 xo_ref, hp_ref, eid_ref, wt_ref, rank_ref, cnt_ref,
                 carry_ref, pad_ref, win_ref):
    b = pl.program_id(0)
    n = pl.program_id(1)
    seq, ci = a_ref.shape
    bm, d = x_ref.shape
    taps = dw_ref.shape[0]
    halo = CONV_HALO

    @pl.when((b == 0) & (n == 0))
    def _():
        carry_ref[...] = jnp.zeros_like(carry_ref)

    @pl.when(n == 0)
    def _():
        pad_ref[0:halo, :] = jnp.zeros((halo, ci), F32)
        pad_ref[halo + seq:, :] = jnp.zeros((halo, ci), F32)
        pad_ref[halo:halo + seq, :] = a_ref[...].astype(F32)

    win_ref[...] = pad_ref[pl.ds(pl.multiple_of(n * bm, bm), bm + 2 * halo), :]
    off = halo - taps // 2
    acc = win_ref[off:off + bm, :] * dw_ref[0:1, :]
    for k in range(1, taps):
        acc = acc + win_ref[off + k:off + k + bm, :] * dw_ref[k:k + 1, :]
    acc = acc + db_ref[...]
    mu = jnp.mean(acc, axis=-1, keepdims=True)
    cen = acc - mu
    var = jnp.mean(cen * cen, axis=-1, keepdims=True)
    yn = cen * lax.rsqrt(var + EPS) * lg_ref[...] + lb_ref[...]
    act = yn * _sigmoid(yn)
    y = _dot(act.astype(BF16), w_ref[...]) + pb_ref[...]
    m = mod_ref[0]
    xn = x_ref[...] + m[:, 2 * d:3 * d] * y
    xo_ref[...] = xn
    _ffn_prologue(xn, m, fg_ref, rwh_ref, rwl_ref, rb_ref, tri_ref,
                  hp_ref, eid_ref, wt_ref, rank_ref, cnt_ref, carry_ref)


def _conv_pw2(a, x, mod3, dw, db, lg, lb, w, pb, fgain, rwh, rwl, rb, tri, batch, seq):
    t, d = x.shape
    ci = a.shape[1]
    bm = ROW_BLOCK
    nlb = seq // bm
    n_exp = rb.shape[0]
    row = lambda b, n: (b * nlb + n, 0)
    full = lambda b, n: (0, 0)
    r_in, r_out = _router_specs(d, n_exp, bm, full, (row, lambda b, n: (0, b * nlb + n)))
    return pl.pallas_call(
        _conv_kernel,
        grid=(batch, nlb),
        in_specs=[
            pl.BlockSpec((seq, ci), lambda b, n: (b, 0)),
            pl.BlockSpec((bm, d), row),
            pl.BlockSpec((1, 1, mod3.shape[2]), lambda b, n: (b, 0, 0)),
            pl.BlockSpec(dw.shape, full),
            pl.BlockSpec((1, ci), full),
            pl.BlockSpec((1, ci), full),
            pl.BlockSpec((1, ci), full),
            pl.BlockSpec(w.shape, full),
            pl.BlockSpec((1, d), full),
        ] + r_in,
        out_specs=[pl.BlockSpec((bm, d), row)] + r_out,
        out_shape=[jax.ShapeDtypeStruct((t, d), F32)] + _router_out_shapes(t, d, n_exp),
        scratch_shapes=[
            pltpu.VMEM((n_exp, LANES), F32),
            pltpu.VMEM((seq + 2 * CONV_HALO, ci), F32),
            pltpu.VMEM((bm + 2 * CONV_HALO, ci), F32),
        ],
        compiler_params=_cparams(("arbitrary", "arbitrary")),
        name="conformer_conv_pw2_router",
    )(a, x, mod3, dw, db, lg, lb, w, pb, fgain, rwh, rwl, rb, tri)


def _rope_tables(seq):
    half = ROPE_DIM // 2
    inv_freq = ROPE_THETA ** (-jnp.arange(half, dtype=F32) * 2.0 / ROPE_DIM)
    ang = jnp.arange(seq, dtype=F32)[:, None] * inv_freq[None, :]
    cos, sin = jnp.cos(ang), jnp.sin(ang)
    dim = jnp.arange(N_HEADS * HEAD_DIM) % HEAD_DIM
    idx = dim % half
    c = jnp.where(dim < ROPE_DIM, cos[:, idx], 1.0)
    sa = jnp.where((dim >= half) & (dim < ROPE_DIM), sin[:, idx], 0.0)
    sb = jnp.where(dim < half, -sin[:, idx], 0.0)
    return c, sa, sb


def _dft_tables(seq):
    n = 2 * seq
    k = jnp.arange(seq, dtype=I32)
    ph = (k[:, None] * k[None, :]) % n
    ang = ph.astype(F32) * (2.0 * math.pi / n)
    return jnp.cos(ang).astype(BF16), jnp.sin(ang).astype(BF16)


def _filter_tables(seq, width, feat_pad):
    t = jnp.arange(seq, dtype=F32) / seq
    freqs = jnp.arange(1, POS_FREQS + 1, dtype=F32)
    ang = 2.0 * math.pi * t[:, None] * freqs[None, :]
    z = jnp.concatenate([t[:, None], jnp.cos(ang), jnp.sin(ang)], axis=-1)
    z = jnp.pad(z, ((0, 0), (0, feat_pad - z.shape[1])))
    deltas = jnp.abs(jnp.linspace(math.log(DECAY_TARGET) / FAST_DECAY_PCT,
                                  math.log(DECAY_TARGET) / SLOW_DECAY_PCT, width, dtype=F32))
    window = jnp.exp(-t[:, None] * deltas[None, :]) + DECAY_SHIFT
    return z, window


def _split_bf16(w):
    hi = w.astype(BF16)
    lo = (w - hi.astype(F32)).astype(BF16)
    return hi, lo


def _moe(hp, eid, wts, rank, counts, x, mod3, wg, wu, wd, seq):
    t = x.shape[0]
    n_exp = wg.shape[0]
    be = EXPERT_BLOCK
    cnt = counts[:, 0].astype(I32)
    pcnt = (cnt + be - 1) // be * be
    pends = jnp.cumsum(pcnt)
    pstart = pends - pcnt
    sel = eid[None, :, :] == jnp.arange(n_exp, dtype=I32)[:, None, None]
    dest = rank + jnp.sum(jnp.where(sel, pstart[:, None, None], 0), axis=0)
    nblk = pl.cdiv(2 * t, be) + n_exp
    blk_e = jnp.minimum(jnp.searchsorted(pends, jnp.arange(nblk, dtype=I32) * be, side="right"),
                        n_exp - 1).astype(I32)
    n_used = (pends[-1:] // be).astype(I32)
    mb = MOVE_BLOCK
    dest3 = dest.reshape(2, t // mb, mb).transpose(1, 0, 2)
    xb = _dispatch(dest3, hp, nblk * be)
    yb = _experts(blk_e, n_used, xb, wg, wu, wd)
    return _combine(dest3, yb, x, wts, mod3, seq)


def kernel(x_prompt, x_sample, c_prompt, c_sample, mix_norm, ffn_norm, ada_w, ada_b, ab_w_in, ab_w_out, q_norm, k_norm, attn_sink, hy_conv_w, hy_conv_b, hy_f_w1, hy_f_b1, hy_f_freq1, hy_f_w2, hy_f_b2, hy_f_freq2, hy_f_w3, hy_skip, cv_pw1_w, cv_pw1_b, cv_dw_w, cv_dw_b, cv_ln_g, cv_ln_b, cv_pw2_w, cv_pw2_b, router_w, router_bias, e_w_gate, e_w_up, e_w_down):
    bp, seq, d = x_prompt.shape
    bs = x_sample.shape[0]
    batch = bp + bs
    t = batch * seq
    depth = ada_w.shape[0]
    n_exp = router_w.shape[1]
    aw = N_HEADS * HEAD_DIM
    assert seq % ROW_BLOCK == 0 and seq >= 3 * ATTN_BLOCK and x_sample.shape[1] == seq

    x = jnp.concatenate([x_prompt, x_sample], axis=0).reshape(t, d)
    c = jnp.concatenate([c_prompt, c_sample], axis=0)
    mod = _ada(c, ada_w, ada_b)

    rc, rsa, rsb = _rope_tables(seq)
    head = jnp.arange(aw) // HEAD_DIM
    bd = jnp.where(head[:, None] == head[None, :], 1.0 / HEAD_DIM, 0.0).astype(BF16)
    cm, sm = _dft_tables(seq)
    feat_pad = 40
    hwidth = hy_skip.shape[2]
    zf, window = _filter_tables(seq, hwidth, feat_pad)
    rw_pad = jnp.pad(router_w, ((0, 0), (0, LANES - n_exp)))
    rwh, rwl = _split_bf16(rw_pad)
    rb = jnp.broadcast_to(router_bias.astype(F32)[:, None], (n_exp, LANES))
    ar = jnp.arange(2 * ROW_BLOCK)
    tri = (ar[:, None] < ar[None, :]).astype(BF16)

    for i in range(depth):
        mod3 = mod[i].reshape(batch, 1, 6 * d)
        j = i // 2
        fgain = ffn_norm[i].reshape(1, d)
        mgain = mix_norm[i].reshape(1, d)
        if i % 2 == 0:
            hidden = hy_f_w1.shape[2]
            order = hy_skip.shape[1]
            w1 = jnp.pad(hy_f_w1[j], ((0, feat_pad - hy_f_w1.shape[1]), (0, 0)))
            w3 = hy_f_w3[j].reshape(hidden, order, 2, hwidth).transpose(1, 2, 0, 3)
            kc, ks, kn = _hyena_spectra(
                zf, w1, hy_f_b1[j].reshape(1, hidden), hy_f_freq1[j].reshape(1, hidden),
                hy_f_w2[j], hy_f_b2[j].reshape(1, hidden), hy_f_freq2[j].reshape(1, hidden),
                w3, window, cm, sm)
            w_in = ab_w_in[j]
            kvw = N_KV_HEADS * HEAD_DIM

            def dup_heads(wpart):
                w4 = wpart.reshape(d, N_KV_HEADS, 1, HEAD_DIM)
                return jnp.broadcast_to(w4, (d, N_KV_HEADS, 2, HEAD_DIM)).reshape(d, KV_DUP_WIDTH)

            w_in = jnp.concatenate([w_in[:, :aw], dup_heads(w_in[:, aw:aw + kvw]),
                                    dup_heads(w_in[:, aw + kvw:aw + 2 * kvw]),
                                    w_in[:, aw + 2 * kvw:]], axis=1).astype(BF16)
            q, k, v, hy = _inproj(
                x, mod3, mgain, w_in,
                jnp.tile(q_norm[j], N_HEADS).reshape(1, aw),
                jnp.tile(k_norm[j], 2 * N_KV_HEADS).reshape(1, KV_DUP_WIDTH),
                bd, rc, rsa, rsb, seq)
            ya = _attention(q, k, v, attn_sink[j], batch, seq)
            z = _hyena(hy, hy_conv_w[j], hy_conv_b[j].reshape(1, -1), cm, sm, kc, ks, kn,
                       hy_skip[j], batch, seq)
            x, hp, eid, wts, rank, counts = _outproj(
                ya, z, x, mod3, ab_w_out[j].astype(BF16), fgain, rwh, rwl, rb, tri, seq)
        else:
            a = _pw1(x, mod3, mgain, cv_pw1_w[j].astype(BF16), cv_pw1_b[j].reshape(1, -1), seq)
            x, hp, eid, wts, rank, counts = _conv_pw2(
                a, x, mod3, cv_dw_w[j], cv_dw_b[j].reshape(1, -1), cv_ln_g[j].reshape(1, -1),
                cv_ln_b[j].reshape(1, -1), cv_pw2_w[j].astype(BF16), cv_pw2_b[j].reshape(1, -1),
                fgain, rwh, rwl, rb, tri, batch, seq)
        x = _moe(hp, eid, wts, rank, counts, x, mod3,
                 e_w_gate[i].astype(BF16), e_w_up[i].astype(BF16), e_w_down[i].astype(BF16), seq)

    y = x.reshape(batch, seq, d)
    return (y[:bp], y[bp:])
```

```python
import functools
import math

import jax
import jax.numpy as jnp
from jax import lax
from jax.experimental import pallas as pl
from jax.experimental.pallas import tpu as pltpu

F32 = jnp.float32
BF16 = jnp.bfloat16
I32 = jnp.int32
U32 = jnp.uint32

N_HEADS = 8
N_KV_HEADS = 2
HEAD_DIM = 64
WINDOW = 128
ATTN_BLOCK = 128
ROPE_DIM = HEAD_DIM // 4
ROPE_THETA = 500000.0
NEG_INF = -1e30
POS_FREQS = 16
FAST_DECAY_PCT = 0.3
SLOW_DECAY_PCT = 1.5
DECAY_TARGET = 0.01
DECAY_SHIFT = 0.05
N_GROUPS = 8
EPS = 1e-6

LANES = 128
ROW_BLOCK = 512
EXPERT_BLOCK = 256
MOVE_BLOCK = 256
HYENA_CBLOCK = 256
HYENA_FBLOCK = 512
KV_DUP_WIDTH =2 * N_KV_HEADS * HEAD_DIM
VMEM_LIMIT = 56 * 1024 * 1024

_HI = lax.Precision.HIGHEST


def _dot(a, b, precision=None):
    return jnp.dot(a, b, preferred_element_type=F32, precision=precision)


def _dot_nt(a, b):
    return lax.dot_general(a, b, (((1,), (1,)), ((), ())), preferred_element_type=F32)


def _sigmoid(x):
    return 1.0 / (1.0 + jnp.exp(-x))


def _rms_mod(x, gain, shift, scale):
    ms = jnp.mean(x * x, axis=-1, keepdims=True)
    y = x * lax.rsqrt(ms + EPS) * gain
    return y * (1.0 + scale) + shift


def _cparams(sem, vmem=VMEM_LIMIT):
    return pltpu.CompilerParams(dimension_semantics=sem, vmem_limit_bytes=vmem)


def _ada_kernel(c_ref, w_ref, b_ref, o_ref):
    c = c_ref[...]
    ca = c * _sigmoid(c)
    o_ref[0] = _dot(ca, w_ref[0], _HI) + b_ref[0]


def _ada(c, ada_w, ada_b):
    depth, d, n = ada_w.shape
    b = c.shape[0]
    tn = n // 4
    return pl.pallas_call(
        _ada_kernel,
        grid=(depth, n // tn),
        in_specs=[
            pl.BlockSpec((b, d), lambda i, j: (0, 0)),
            pl.BlockSpec((1, d, tn), lambda i, j: (i, 0, j)),
            pl.BlockSpec((1, 1, tn), lambda i, j: (i, 0, j)),
        ],
        out_specs=pl.BlockSpec((1, b, tn), lambda i, j: (i, 0, j)),
        out_shape=jax.ShapeDtypeStruct((depth, b, n), F32),
        compiler_params=_cparams(("arbitrary", "arbitrary")),
        name="ada_mod",
    )(c, ada_w, ada_b.reshape(depth, 1, n))


def _headnorm_rope(t, gain, bd, c, sa, sb):
    ms = _dot((t * t).astype(BF16), bd)
    tn = t * lax.rsqrt(ms + EPS) * gain
    n = t.shape[1]
    half = ROPE_DIM // 2
    return tn * c + pltpu.roll(tn, half, 1) * sa + pltpu.roll(tn, n - half, 1) * sb


def _two_group_specs(bm, d, n_first):
    return [pl.BlockSpec((bm, d), lambda i: (jnp.minimum(i, n_first - 1), 0)),
            pl.BlockSpec((bm, d), lambda i: (jnp.maximum(i - n_first, 0), 0))]


def _inproj_kernel(n_first, xa_ref, xb_ref, mod_ref, g_ref, w_ref, gq_ref, gk_ref, bd_ref, c_ref, sa_ref,
                   sb_ref, q_ref, k_ref, v_ref, hy_ref):
    d = xa_ref.shape[1]
    aw = N_HEADS * HEAD_DIM
    kw = KV_DUP_WIDTH
    m = mod_ref[0]
    x = jnp.where(pl.program_id(0) < n_first, xa_ref[...], xb_ref[...])
    h = _rms_mod(x, g_ref[...], m[:, 0:d], m[:, d:2 * d])
    proj = _dot(h.astype(BF16), w_ref[...])
    q = proj[:, :aw]
    k = proj[:, aw:aw + kw]
    c, sa, sb = c_ref[...], sa_ref[...], sb_ref[...]
    bd = bd_ref[...]
    qr = _headnorm_rope(q, gq_ref[...], bd, c, sa, sb)
    kr = _headnorm_rope(k, gk_ref[...], bd[:kw, :kw], c[:, :kw], sa[:, :kw], sb[:, :kw])
    q_ref[...] = (qr * (HEAD_DIM ** -0.5)).astype(BF16)
    k_ref[...] = kr.astype(BF16)
    v_ref[...] = proj[:, aw + kw:aw + 2 * kw].astype(BF16)
    hy_ref[...] = proj[:, aw + 2 * kw:].astype(BF16)


def _inproj(xa, xb, mod3, gain, w, gq, gk, bd, rc, rsa, rsb, seq):
    d = xa.shape[1]
    t = xa.shape[0] + xb.shape[0]
    bm = ROW_BLOCK
    nlb = seq // bm
    aw = N_HEADS * HEAD_DIM
    kw = KV_DUP_WIDTH
    hw = w.shape[1] - aw - 2 * kw
    row = lambda i: (i, 0)
    full = lambda i: (0, 0)
    tab = lambda i: (i % nlb, 0)
    n_first = xa.shape[0] // bm
    return pl.pallas_call(
        functools.partial(_inproj_kernel, n_first),
        grid=(t // bm,),
        in_specs=_two_group_specs(bm, d, n_first) + [
            pl.BlockSpec((1, 1, mod3.shape[2]), lambda i: (i // nlb, 0, 0)),
            pl.BlockSpec((1, d), full),
            pl.BlockSpec(w.shape, full),
            pl.BlockSpec((1, aw), full),
            pl.BlockSpec((1, kw), full),
            pl.BlockSpec((aw, aw), full),
            pl.BlockSpec((bm, aw), tab),
            pl.BlockSpec((bm, aw), tab),
            pl.BlockSpec((bm, aw), tab),
        ],
        out_specs=[
            pl.BlockSpec((bm, aw), row),
            pl.BlockSpec((bm, kw), row),
            pl.BlockSpec((bm, kw), row),
            pl.BlockSpec((bm, hw), row),
        ],
        out_shape=[
            jax.ShapeDtypeStruct((t, aw), BF16),
            jax.ShapeDtypeStruct((t, kw), BF16),
            jax.ShapeDtypeStruct((t, kw), BF16),
            jax.ShapeDtypeStruct((t, hw), BF16),
        ],
        compiler_params=_cparams(("arbitrary",)),
        name="inproj",
    )(xa, xb, mod3, gain, w, gq, gk, bd, rc, rsa, rsb)


def _attn_kernel(sink_ref, q_ref, k_ref, v_ref, o_ref):
    n = pl.program_id(1)
    seq = k_ref.shape[0]
    blk = ATTN_BLOCK
    span = 3 * blk
    ws = pl.multiple_of(jnp.clip((n - 1) * blk, 0, seq - span), blk)
    kwin = k_ref[pl.ds(ws, span), :]
    vwin = v_ref[pl.ds(ws, span), :]
    pair = 2 * HEAD_DIM
    low = lax.broadcasted_iota(I32, (span, pair), 1) < HEAD_DIM
    zero = jnp.zeros((span, pair), kwin.dtype)

    def halves(t, g):
        tg = t[:, g * pair:(g + 1) * pair]
        return jnp.where(low, tg, zero), jnp.where(low, zero, tg)

    ks = tuple(halves(kwin, g) for g in range(N_KV_HEADS))
    vs = tuple(halves(vwin, g) for g in range(N_KV_HEADS))
    qpos = n * blk + lax.broadcasted_iota(I32, (blk, span), 0)
    kpos = ws + lax.broadcasted_iota(I32, (blk, span), 1)
    valid = jnp.abs(kpos - qpos) <= WINDOW
    q_group = N_HEADS // N_KV_HEADS
    for p in range(N_HEADS // 2):
        qp = q_ref[:, p * 2 * HEAD_DIM:(p + 1) * 2 * HEAD_DIM]
        acc = None
        for r in range(2):
            hd = 2 * p + r
            g = hd // q_group
            s = _dot_nt(qp, ks[g][r])
            s = jnp.where(valid, s, NEG_INF)
            sk = sink_ref[hd]
            mx = jnp.maximum(jnp.max(s, axis=-1, keepdims=True), sk)
            e = jnp.exp(s - mx)
            den = jnp.sum(e, axis=-1, keepdims=True) + jnp.exp(sk - mx)
            pr = (e / den).astype(BF16)
            o = _dot(pr, vs[g][r])
            acc = o if acc is None else acc + o
        o_ref[:, p * 2 * HEAD_DIM:(p + 1) * 2 * HEAD_DIM] = acc.astype(BF16)


def _attention(q, k, v, sink, batch, seq):
    t, aw = q.shape
    kw = k.shape[1]
    nb = seq // ATTN_BLOCK
    return pl.pallas_call(
        _attn_kernel,
        grid=(batch, nb),
        in_specs=[
            pl.BlockSpec(memory_space=pltpu.SMEM),
            pl.BlockSpec((ATTN_BLOCK, aw), lambda b, n: (b * nb + n, 0)),
            pl.BlockSpec((seq, kw), lambda b, n: (b, 0)),
            pl.BlockSpec((seq, kw), lambda b, n: (b, 0)),
        ],
        out_specs=pl.BlockSpec((ATTN_BLOCK, aw), lambda b, n: (b * nb + n, 0)),
        out_shape=jax.ShapeDtypeStruct((t, aw), BF16),
        compiler_params=_cparams(("arbitrary", "arbitrary")),
        name="window_attn",
    )(sink, q, k, v)


def _split_dot(mat, v):
    vh = v.astype(BF16)
    vl = (v - vh.astype(F32)).astype(BF16)
    return _dot(mat, vh) + _dot(mat, vl)


def _filter_kernel(z_ref, w1_ref, b1_ref, f1_ref, w2_ref, b2_ref, f2_ref, w3_ref, win_ref,
                   cm_ref, sm_ref, kc_ref, ks_ref, kn_ref):
    h1 = jnp.sin(f1_ref[...] * (_dot(z_ref[...], w1_ref[...], _HI) + b1_ref[...]))
    h2 = jnp.sin(f2_ref[...] * (_dot(h1, w2_ref[...], _HI) + b2_ref[...]))
    win = win_ref[...]
    hf = _dot(h2, w3_ref[0, 0], _HI) * win
    hg = _dot(h2, w3_ref[0, 1], _HI) * win
    row = lax.broadcasted_iota(I32, hf.shape, 0)
    hg = jnp.where(row == 0, 0.0, hg)
    nrm = jnp.sum(jnp.abs(hf), axis=0, keepdims=True) + jnp.sum(jnp.abs(hg), axis=0, keepdims=True)
    hf = hf / nrm
    hg = hg / nrm
    even = hf + hg
    odd = hf - hg
    kc_ref[0] = _split_dot(cm_ref[...], even)
    ks_ref[0] = _split_dot(sm_ref[...], odd)
    alt = (1 - 2 * (row & 1)).astype(F32)
    kn_ref[0] = jnp.sum(alt * even, axis=0, keepdims=True)


def _single(shape, index_map):
    return pl.BlockSpec(shape, index_map, pipeline_mode=pl.Buffered(1))


def _hyena_spectra(zf, w1, b1, f1, w2, b2, f2, w3, win, cm, sm):
    seq = zf.shape[0]
    order, _, hidden, width = w3.shape
    cb = HYENA_CBLOCK
    full2 = lambda o, c: (0, 0)
    return pl.pallas_call(
        _filter_kernel,
        grid=(order, width // cb),
        in_specs=[
            pl.BlockSpec(zf.shape, full2),
            pl.BlockSpec(w1.shape, full2),
            pl.BlockSpec(b1.shape, full2),
            pl.BlockSpec(f1.shape, full2),
            pl.BlockSpec(w2.shape, full2),
            pl.BlockSpec(b2.shape, full2),
            pl.BlockSpec(f2.shape, full2),
            pl.BlockSpec((1, 2, hidden, cb), lambda o, c: (o, 0, 0, c)),
            pl.BlockSpec((seq, cb), lambda o, c: (0, c)),
            _single((seq, seq), full2),
            _single((seq, seq), full2),
        ],
        out_specs=[
            pl.BlockSpec((1, seq, cb), lambda o, c: (o, 0, c)),
            pl.BlockSpec((1, seq, cb), lambda o, c: (o, 0, c)),
            pl.BlockSpec((1, 1, cb), lambda o, c: (o, 0, c)),
        ],
        out_shape=[
            jax.ShapeDtypeStruct((order, seq, width), F32),
            jax.ShapeDtypeStruct((order, seq, width), F32),
            jax.ShapeDtypeStruct((order, 1, width), F32),
        ],
        compiler_params=_cparams(("arbitrary", "arbitrary")),
        name="hyena_spectra",
    )(zf, w1, b1, f1, w2, b2, f2, w3, win, cm, sm)


def _hyena_kernel(x1_ref, x2_ref, v_ref, cw1_ref, cw2_ref, cwv_ref, cb1_ref, cb2_ref, cbv_ref,
                  cm_ref, sm_ref, kc_ref, ks_ref, kn_ref, skip_ref, o_ref, yc_ref, ys_ref):
    seq = x1_ref.shape[0]
    row = lax.broadcasted_iota(I32, (seq, 1), 0)
    alt = (1 - 2 * (row & 1)).astype(F32)
    inv_n = 1.0 / (2 * seq)

    def short_conv(ref, w_ref, b_ref):
        x = ref[...].astype(F32)
        w = w_ref[...]
        prev = jnp.where(row == 0, 0.0, pltpu.roll(x, 1, 0))
        nxt = jnp.where(row == seq - 1, 0.0, pltpu.roll(x, seq - 1, 0))
        return prev * w[0:1] + x * w[1:2] + nxt * w[2:3] + b_ref[...]

    fblk = min(seq, HYENA_FBLOCK)

    def long_conv(u, o):
        ub = u.astype(BF16)
        zn = jnp.sum(alt * u, axis=0, keepdims=True)
        for f0 in range(0, seq, fblk):
            fs = slice(f0, f0 + fblk)
            zc = _dot(cm_ref[fs, :], ub)
            zs = _dot(sm_ref[fs, :], ub)
            kc = kc_ref[o, fs, :]
            ks = ks_ref[o, fs, :]
            yc = zc * kc - zs * ks
            if f0 == 0:
                yc = jnp.where(row[fs] == 0, 0.5 * yc, yc)
            yc_ref[fs, :] = yc.astype(BF16)
            ys_ref[fs, :] = (zc * ks + zs * kc).astype(BF16)
        y = _dot(cm_ref[...], yc_ref[...]) + _dot(sm_ref[...], ys_ref[...])
        y = (2.0 * y + alt * (zn * kn_ref[o])) * inv_n
        return y + u * skip_ref[o:o + 1]

    z = short_conv(v_ref, cwv_ref, cbv_ref)
    z = short_conv(x1_ref, cw1_ref, cb1_ref) * long_conv(z, 0)
    z = short_conv(x2_ref, cw2_ref, cb2_ref) * long_conv(z, 1)
    o_ref[...] = z.astype(BF16)


def _hyena(hy, conv_w, conv_b, cm, sm, kc, ks, kn, skip, batch, seq):
    t = hy.shape[0]
    width = hy.shape[1] // 3
    cb = HYENA_CBLOCK
    ncb = width // cb
    order = kc.shape[0]
    taps = conv_w.shape[0]
    full2 = lambda c, b: (0, 0)

    def col(k):
        return lambda c, b: (b, k * ncb + c)

    def wcol(k):
        return lambda c, b: (0, k * ncb + c)

    return pl.pallas_call(
        _hyena_kernel,
        grid=(ncb, batch),
        in_specs=[
            pl.BlockSpec((seq, cb), col(0)),
            pl.BlockSpec((seq, cb), col(1)),
            pl.BlockSpec((seq, cb), col(2)),
            pl.BlockSpec((taps, cb), wcol(0)),
            pl.BlockSpec((taps, cb), wcol(1)),
            pl.BlockSpec((taps, cb), wcol(2)),
            pl.BlockSpec((1, cb), wcol(0)),
            pl.BlockSpec((1, cb), wcol(1)),
            pl.BlockSpec((1, cb), wcol(2)),
            _single((seq, seq), full2),
            _single((seq, seq), full2),
            _single((order, seq, cb), lambda c, b: (0, 0, c)),
            _single((order, seq, cb), lambda c, b: (0, 0, c)),
            pl.BlockSpec((order, 1, cb), lambda c, b: (0, 0, c)),
            pl.BlockSpec((order, cb), lambda c, b: (0, c)),
        ],
        out_specs=pl.BlockSpec((seq, cb), lambda c, b: (b, c)),
        out_shape=jax.ShapeDtypeStruct((t, width), BF16),
        scratch_shapes=[pltpu.VMEM((seq, cb), BF16), pltpu.VMEM((seq, cb), BF16)],
        compiler_params=_cparams(("arbitrary", "arbitrary")),
        name="hyena",
    )(hy, hy, hy, conv_w, conv_w, conv_w, conv_b, conv_b, conv_b, cm, sm, kc, ks, kn, skip)


def _ffn_prologue(xn, m, fg_ref, rwh_ref, rwl_ref, rb_ref, tri_ref,
                  hp_ref, eid_ref, wt_ref, rank_ref, cnt_ref, carry_ref):
    bm, d = xn.shape
    n_exp = rb_ref.shape[0]
    epg = n_exp // N_GROUPS
    hf = _rms_mod(xn, fg_ref[...], m[:, 3 * d:4 * d], m[:, 4 * d:5 * d])
    hb = hf.astype(BF16)
    hbf = hb.astype(F32)
    lo = pltpu.bitcast(hbf[:, :d // 2], U32) >> 16
    hi = pltpu.bitcast(hbf[:, d // 2:], U32) & jnp.uint32(0xFFFF0000)
    hp_ref[...] = lo | hi

    hl = (hf - hbf).astype(BF16)
    lg = _dot(hb, rwh_ref[...]) + _dot(hl, rwh_ref[...]) + _dot(hb, rwl_ref[...])
    lt = lg.T[:n_exp]
    s = _sigmoid(lt)
    sb = s + rb_ref[:, 0:1]
    s3 = s.reshape(N_GROUPS, epg, bm)
    sb3 = sb.reshape(N_GROUPS, epg, bm)
    ji = lax.broadcasted_iota(I32, sb3.shape, 1)
    gi = lax.broadcasted_iota(I32, sb3.shape, 0)
    ninf = -jnp.inf
    m1 = jnp.max(sb3, axis=1, keepdims=True)
    i1 = jnp.min(jnp.where(sb3 == m1, ji, epg), axis=1, keepdims=True)
    m2 = jnp.max(jnp.where(ji == i1, ninf, sb3), axis=1, keepdims=True)
    gs = m1 + m2
    gm = jnp.max(gs, axis=0, keepdims=True)
    gidx = lax.broadcasted_iota(I32, gs.shape, 0)
    gbest = jnp.min(jnp.where(gs == gm, gidx, N_GROUPS), axis=0, keepdims=True)
    pick = gi == gbest
    sel = jnp.max(jnp.where(pick, sb3, ninf), axis=0)
    ssel = jnp.max(jnp.where(pick, s3, ninf), axis=0)
    jj = lax.broadcasted_iota(I32, sel.shape, 0)
    a1 = jnp.max(sel, axis=0, keepdims=True)
    j1 = jnp.min(jnp.where(sel == a1, jj, epg), axis=0, keepdims=True)
    sel2 = jnp.where(jj == j1, ninf, sel)
    a2 = jnp.max(sel2, axis=0, keepdims=True)
    j2 = jnp.min(jnp.where(sel2 == a2, jj, epg), axis=0, keepdims=True)
    w1 = jnp.sum(jnp.where(jj == j1, ssel, 0.0), axis=0, keepdims=True)
    w2 = jnp.sum(jnp.where(jj == j2, ssel, 0.0), axis=0, keepdims=True)
    den = w1 + w2
    gb = gbest[0]
    e1 = gb * epg + j1
    e2 = gb * epg + j2
    eid_ref[0:1, :] = e1
    eid_ref[1:2, :] = e2
    wt_ref[0:1, :] = w1 / den
    wt_ref[1:2, :] = w2 / den

    ei = lax.broadcasted_iota(I32, (n_exp, bm), 0)
    oh = jnp.concatenate([jnp.where(ei == e1, 1.0, 0.0), jnp.where(ei == e2, 1.0, 0.0)], axis=1)
    pre = _dot(oh.astype(BF16), tri_ref[...])
    carry = carry_ref[...]
    rk = jnp.sum(oh * (pre + carry[:, 0:1]), axis=0, keepdims=True)
    rank_ref[0:1, :] = rk[:, :bm].astype(I32)
    rank_ref[1:2, :] = rk[:, bm:].astype(I32)
    carry = carry + jnp.sum(oh, axis=1, keepdims=True)
    carry_ref[...] = carry
    cnt_ref[...] = carry


def _outproj_kernel(n_first, ya_ref, z_ref, xa_ref, xb_ref, mod_ref, w_ref, fg_ref, rwh_ref, rwl_ref, rb_ref,
                    tri_ref, xo_ref, hp_ref, eid_ref, wt_ref, rank_ref, cnt_ref, carry_ref):
    @pl.when(pl.program_id(0) == 0)
    def _():
        carry_ref[...] = jnp.zeros_like(carry_ref)

    d = xa_ref.shape[1]
    aw = ya_ref.shape[1]
    m = mod_ref[0]
    y = _dot(ya_ref[...], w_ref[:aw, :]) + _dot(z_ref[...], w_ref[aw:, :])
    x = jnp.where(pl.program_id(0) < n_first, xa_ref[...], xb_ref[...])
    xn = x + m[:, 2 * d:3 * d] * y
    xo_ref[...] = xn
    _ffn_prologue(xn, m, fg_ref, rwh_ref, rwl_ref, rb_ref, tri_ref,
                  hp_ref, eid_ref, wt_ref, rank_ref, cnt_ref, carry_ref)


def _router_specs(d, n_exp, bm, full, tok):
    in_specs = [
        pl.BlockSpec((1, d), full),
        pl.BlockSpec((d, LANES), full),
        pl.BlockSpec((d, LANES), full),
        pl.BlockSpec((n_exp, LANES), full),
        pl.BlockSpec((2 * bm, 2 * bm), full),
    ]
    out_specs = [
        pl.BlockSpec((bm, d // 2), tok[0]),
        pl.BlockSpec((2, bm), tok[1]),
        pl.BlockSpec((2, bm), tok[1]),
        pl.BlockSpec((2, bm), tok[1]),
        pl.BlockSpec((n_exp, LANES), full),
    ]
    return in_specs, out_specs


def _router_out_shapes(t, d, n_exp):
    return [
        jax.ShapeDtypeStruct((t, d // 2), U32),
        jax.ShapeDtypeStruct((2, t), I32),
        jax.ShapeDtypeStruct((2, t), F32),
        jax.ShapeDtypeStruct((2, t), I32),
        jax.ShapeDtypeStruct((n_exp, LANES), F32),
    ]


def _outproj(ya, z, xa, xb, mod3, w, fgain, rwh, rwl, rb, tri, seq):
    d = xa.shape[1]
    t = xa.shape[0] + xb.shape[0]
    bm = ROW_BLOCK
    nlb = seq // bm
    n_exp = rb.shape[0]
    n_first = xa.shape[0] // bm
    row = lambda i: (i, 0)
    full = lambda i: (0, 0)
    r_in, r_out = _router_specs(d, n_exp, bm, full, (row, lambda i: (0, i)))
    return pl.pallas_call(
        functools.partial(_outproj_kernel, n_first),
        grid=(t // bm,),
        in_specs=[
            pl.BlockSpec((bm, ya.shape[1]), row),
            pl.BlockSpec((bm, z.shape[1]), row),
        ] + _two_group_specs(bm, d, n_first) + [
            pl.BlockSpec((1, 1, mod3.shape[2]), lambda i: (i // nlb, 0, 0)),
            pl.BlockSpec(w.shape, full),
        ] + r_in,
        out_specs=[pl.BlockSpec((bm, d), row)] + r_out,
        out_shape=[jax.ShapeDtypeStruct((t, d), F32)] + _router_out_shapes(t, d, n_exp),
        scratch_shapes=[pltpu.VMEM((n_exp, LANES), F32)],
        compiler_params=_cparams(("arbitrary",)),
        name="outproj_router",
    )(ya, z, xa, xb, mod3, w, fgain, rwh, rwl, rb, tri)


def _dispatch_kernel(dest_hbm, h_ref, xb_in, xb_hbm, dest_smem, isem, sem):
    del xb_in
    i = pl.program_id(0)
    bm = h_ref.shape[0]
    cp = pltpu.make_async_copy(dest_hbm.at[i], dest_smem, isem)
    cp.start()
    cp.wait()

    def row_copy(t, dst):
        return pltpu.make_async_copy(h_ref.at[pl.ds(t, 1)], xb_hbm.at[pl.ds(dst, 1)], sem)

    def issue(t, carry):
        row_copy(t, dest_smem[0, t]).start()
        row_copy(t, dest_smem[1, t]).start()
        return carry

    lax.fori_loop(0, bm, issue, 0, unroll=8)

    def drain(t, carry):
        row_copy(0, 0).wait()
        row_copy(0, 0).wait()
        return carry

    lax.fori_loop(0, bm, drain, 0, unroll=8)


def _dispatch(dest3, hp, n_slots):
    t, dw = hp.shape
    bm = MOVE_BLOCK
    zeros = jnp.zeros((n_slots, dw), U32)
    return pl.pallas_call(
        _dispatch_kernel,
        grid=(t // bm,),
        in_specs=[
            pl.BlockSpec(memory_space=pl.ANY),
            pl.BlockSpec((bm, dw), lambda i: (i, 0)),
            pl.BlockSpec(memory_space=pl.ANY),
        ],
        out_specs=pl.BlockSpec(memory_space=pl.ANY),
        out_shape=jax.ShapeDtypeStruct((n_slots, dw), U32),
        scratch_shapes=[
            pltpu.SMEM((2, bm), I32),
            pltpu.SemaphoreType.DMA,
            pltpu.SemaphoreType.DMA,
        ],
        input_output_aliases={2: 0},
        compiler_params=_cparams(("arbitrary",)),
        name="moe_dispatch",
    )(dest3, hp, zeros)


def _expert_kernel(be_ref, nu_ref, x_ref, wg_ref, wu_ref, wd_ref, o_ref):
    j = pl.program_id(0)

    @pl.when(j < nu_ref[0])
    def _():
        xp = x_ref[...]
        half = xp.shape[1]
        a = pltpu.bitcast(xp << 16, F32).astype(BF16)
        b = pltpu.bitcast(xp & jnp.uint32(0xFFFF0000), F32).astype(BF16)
        def up(w_ref):
            return (_dot(a, w_ref[0, :half, :].astype(BF16)) +
                    _dot(b, w_ref[0, half:, :].astype(BF16)))

        g = up(wg_ref)
        u = up(wu_ref)
        hm = (g * _sigmoid(g) * u).astype(BF16)
        o_ref[...] = _dot(hm, wd_ref[0].astype(BF16))

    @pl.when(j >= nu_ref[0])
    def _():
        o_ref[...] = jnp.zeros_like(o_ref)


def _experts(blk_e, n_used, xb, wg, wu, wd):
    n_slots, dw = xb.shape
    _, d, de = wg.shape
    bm = EXPERT_BLOCK
    grid_spec = pltpu.PrefetchScalarGridSpec(
        num_scalar_prefetch=2,
        grid=(n_slots // bm,),
        in_specs=[
            pl.BlockSpec((bm, dw), lambda j, be, nu: (j, 0)),
            pl.BlockSpec((1, d, de), lambda j, be, nu: (be[j], 0, 0)),
            pl.BlockSpec((1, d, de), lambda j, be, nu: (be[j], 0, 0)),
            pl.BlockSpec((1, de, d), lambda j, be, nu: (be[j], 0, 0)),
        ],
        out_specs=pl.BlockSpec((bm, d), lambda j, be, nu: (j, 0)),
    )
    return pl.pallas_call(
        _expert_kernel,
        grid_spec=grid_spec,
        out_shape=jax.ShapeDtypeStruct((n_slots, d), F32),
        compiler_params=_cparams(("arbitrary",)),
        name="moe_experts",
    )(blk_e, n_used, xb, wg, wu, wd)


def _combine_kernel(b0, dest_hbm, yb_hbm, x_ref, wt_ref, mod_ref, o_ref, dest_smem, buf, isem, sem):
    i = b0 + pl.program_id(0)
    bm, d = x_ref.shape
    cp = pltpu.make_async_copy(dest_hbm.at[i], dest_smem, isem)
    cp.start()
    cp.wait()

    def row_copy(k, t, src):
        return pltpu.make_async_copy(yb_hbm.at[pl.ds(src, 1)], buf.at[k, pl.ds(t, 1)], sem)

    def issue(t, carry):
        row_copy(0, t, dest_smem[0, t]).start()
        row_copy(1, t, dest_smem[1, t]).start()
        return carry

    lax.fori_loop(0, bm, issue, 0, unroll=8)

    def drain(t, carry):
        row_copy(0, 0, 0).wait()
        row_copy(1, 0, 0).wait()
        return carry

    lax.fori_loop(0, bm, drain, 0, unroll=8)

    eye = lax.broadcasted_iota(I32, (bm, bm), 0) == lax.broadcasted_iota(I32, (bm, bm), 1)
    w = wt_ref[...]
    w0 = jnp.sum(jnp.where(eye, w[0:1, :], 0.0), axis=1, keepdims=True)
    w1 = jnp.sum(jnp.where(eye, w[1:2, :], 0.0), axis=1, keepdims=True)
    g2 = mod_ref[0][:, 5 * d:6 * d]
    o_ref[...] = x_ref[...] + g2 * (w0 * buf[0] + w1 * buf[1])


def _combine(dest3, yb, x, wts, mod3, seq, row0, nrows):
    d = x.shape[1]
    bm = MOVE_BLOCK
    nlb = seq // bm
    b0 = row0 // bm
    return pl.pallas_call(
        functools.partial(_combine_kernel, b0),
        grid=(nrows // bm,),
        in_specs=[
            pl.BlockSpec(memory_space=pl.ANY),
            pl.BlockSpec(memory_space=pl.ANY),
            pl.BlockSpec((bm, d), lambda i: (b0 + i, 0)),
            pl.BlockSpec((2, bm), lambda i: (0, b0 + i)),
            pl.BlockSpec((1, 1, mod3.shape[2]), lambda i: ((b0 + i) // nlb, 0, 0)),
        ],
        out_specs=pl.BlockSpec((bm, d), lambda i: (i, 0)),
        out_shape=jax.ShapeDtypeStruct((nrows, d), F32),
        scratch_shapes=[
            pltpu.SMEM((2, bm), I32),
            pltpu.VMEM((2, bm, d), F32),
            pltpu.SemaphoreType.DMA,
            pltpu.SemaphoreType.DMA,
        ],
        compiler_params=_cparams(("arbitrary",)),
        name="moe_combine",
    )(dest3, yb, x, wts, mod3)


def _pw1_kernel(x_ref, mod_ref, g_ref, w_ref, b_ref, o_ref):
    d = x_ref.shape[1]
    ci = o_ref.shape[1]
    m = mod_ref[0]
    h = _rms_mod(x_ref[...], g_ref[...], m[:, 0:d], m[:, d:2 * d])
    a = _dot(h.astype(BF16), w_ref[...]) + b_ref[...]
    o_ref[...] = (a[:, :ci] * _sigmoid(a[:, ci:])).astype(BF16)


def _pw1(x, mod3, gain, w, b, seq):
    t, d = x.shape
    bm = ROW_BLOCK
    nlb = seq // bm
    ci = w.shape[1] // 2
    row = lambda i: (i, 0)
    full = lambda i: (0, 0)
    return pl.pallas_call(
        _pw1_kernel,
        grid=(t // bm,),
        in_specs=[
            pl.BlockSpec((bm, d), row),
            pl.BlockSpec((1, 1, mod3.shape[2]), lambda i: (i // nlb, 0, 0)),
            pl.BlockSpec((1, d), full),
            pl.BlockSpec(w.shape, full),
            pl.BlockSpec((1, w.shape[1]), full),
        ],
        out_specs=pl.BlockSpec((bm, ci), row),
        out_shape=jax.ShapeDtypeStruct((t, ci), BF16),
        compiler_params=_cparams(("arbitrary",)),
        name="conformer_pw1",
    )(x, mod3, gain, w, b)


CONV_HALO = 16


def _conv_kernel(a_ref, x_ref, mod_ref, dw_ref, db_ref, lg_ref, lb_ref, w_ref, pb_ref,
                 fg_ref, rwh_ref, rwl_ref, rb_ref, tri_ref,
                 xo_ref, hp_ref, eid_ref, wt_ref, rank_ref, cnt_ref,
                 carry_ref, pad_ref, win_ref):
    b = pl.program_id(0)
    n = pl.program_id(1)
    seq, ci = a_ref.shape
    bm, d = x_ref.shape
    taps = dw_ref.shape[0]
    halo = CONV_HALO

    @pl.when((b == 0) & (n == 0))
    def _():
        carry_ref[...] = jnp.zeros_like(carry_ref)

    @pl.when(n == 0)
    def _():
        pad_ref[0:halo, :] = jnp.zeros((halo, ci), F32)
        pad_ref[halo + seq:, :] = jnp.zeros((halo, ci), F32)
        pad_ref[halo:halo + seq, :] = a_ref[...].astype(F32)

    win_ref[...] = pad_ref[pl.ds(pl.multiple_of(n * bm, bm), bm + 2 * halo), :]
    off = halo - taps // 2
    acc = win_ref[off:off + bm, :] * dw_ref[0:1, :]
    for k in range(1, taps):
        acc = acc + win_ref[off + k:off + k + bm, :] * dw_ref[k:k + 1, :]
    acc = acc + db_ref[...]
    mu = jnp.mean(acc, axis=-1, keepdims=True)
    cen = acc - mu
    var = jnp.mean(cen * cen, axis=-1, keepdims=True)
    yn = cen * lax.rsqrt(var + EPS) * lg_ref[...] + lb_ref[...]
    act = yn * _sigmoid(yn)
    y = _dot(act.astype(BF16), w_ref[...]) + pb_ref[...]
    m = mod_ref[0]
    xn = x_ref[...] + m[:, 2 * d:3 * d] * y
    xo_ref[...] = xn
    _ffn_prologue(xn, m, fg_ref, rwh_ref, rwl_ref, rb_ref, tri_ref,
                  hp_ref, eid_ref, wt_ref, rank_ref, cnt_ref, carry_ref)


def _conv_pw2(a, x, mod3, dw, db, lg, lb, w, pb, fgain, rwh, rwl, rb, tri, batch, seq):
    t, d = x.shape
    ci = a.shape[1]
    bm = ROW_BLOCK
    nlb = seq // bm
    n_exp = rb.shape[0]
    row = lambda b, n: (b * nlb + n, 0)
    full = lambda b, n: (0, 0)
    r_in, r_out = _router_specs(d, n_exp, bm, full, (row, lambda b, n: (0, b * nlb + n)))
    return pl.pallas_call(
        _conv_kernel,
        grid=(batch, nlb),
        in_specs=[
            pl.BlockSpec((seq, ci), lambda b, n: (b, 0)),
            pl.BlockSpec((bm, d), row),
            pl.BlockSpec((1, 1, mod3.shape[2]), lambda b, n: (b, 0, 0)),
            pl.BlockSpec(dw.shape, full),
            pl.BlockSpec((1, ci), full),
            pl.BlockSpec((1, ci), full),
            pl.BlockSpec((1, ci), full),
            pl.BlockSpec(w.shape, full),
            pl.BlockSpec((1, d), full),
        ] + r_in,
        out_specs=[pl.BlockSpec((bm, d), row)] + r_out,
        out_shape=[jax.ShapeDtypeStruct((t, d), F32)] + _router_out_shapes(t, d, n_exp),
        scratch_shapes=[
            pltpu.VMEM((n_exp, LANES), F32),
            pltpu.VMEM((seq + 2 * CONV_HALO, ci), F32),
            pltpu.VMEM((bm + 2 * CONV_HALO, ci), F32),
        ],
        compiler_params=_cparams(("arbitrary", "arbitrary")),
        name="conformer_conv_pw2_router",
    )(a, x, mod3, dw, db, lg, lb, w, pb, fgain, rwh, rwl, rb, tri)


def _rope_tables(seq):
    half = ROPE_DIM // 2
    inv_freq = ROPE_THETA ** (-jnp.arange(half, dtype=F32) * 2.0 / ROPE_DIM)
    ang = jnp.arange(seq, dtype=F32)[:, None] * inv_freq[None, :]
    cos, sin = jnp.cos(ang), jnp.sin(ang)
    dim = jnp.arange(N_HEADS * HEAD_DIM) % HEAD_DIM
    idx = dim % half
    c = jnp.where(dim < ROPE_DIM, cos[:, idx], 1.0)
    sa = jnp.where((dim >= half) & (dim < ROPE_DIM), sin[:, idx], 0.0)
    sb = jnp.where(dim < half, -sin[:, idx], 0.0)
    return c, sa, sb


def _dft_tables(seq):
    n = 2 * seq
    k = jnp.arange(seq, dtype=I32)
    ph = (k[:, None] * k[None, :]) % n
    ang = ph.astype(F32) * (2.0 * math.pi / n)
    return jnp.cos(ang).astype(BF16), jnp.sin(ang).astype(BF16)


def _filter_tables(seq, width, feat_pad):
    t = jnp.arange(seq, dtype=F32) / seq
    freqs = jnp.arange(1, POS_FREQS + 1, dtype=F32)
    ang = 2.0 * math.pi * t[:, None] * freqs[None, :]
    z = jnp.concatenate([t[:, None], jnp.cos(ang), jnp.sin(ang)], axis=-1)
    z = jnp.pad(z, ((0, 0), (0, feat_pad - z.shape[1])))
    deltas = jnp.abs(jnp.linspace(math.log(DECAY_TARGET) / FAST_DECAY_PCT,
                                  math.log(DECAY_TARGET) / SLOW_DECAY_PCT, width, dtype=F32))
    window = jnp.exp(-t[:, None] * deltas[None, :]) + DECAY_SHIFT
    return z, window


def _split_bf16(w):
    hi = w.astype(BF16)
    lo = (w - hi.astype(F32)).astype(BF16)
    return hi, lo


DEST_CHUNK = 32


def _dest_kernel(ps_ref, eid_ref, rank_ref, o_ref):
    eid = eid_ref[...]
    acc = rank_ref[...]
    for e in range(ps_ref.shape[0]):
        acc = acc + jnp.where(eid == e, ps_ref[e], 0)
    mb = o_ref.shape[2]
    for c in range(o_ref.shape[0]):
        o_ref[c] = acc[:, c * mb:(c + 1) * mb]


def _slot_index(pstart, eid, rank):
    t = eid.shape[1]
    mb = MOVE_BLOCK
    chunk = min(DEST_CHUNK, t // mb)
    cw = chunk * mb
    return pl.pallas_call(
        _dest_kernel,
        grid=(t // cw,),
        in_specs=[
            pl.BlockSpec(memory_space=pltpu.SMEM),
            pl.BlockSpec((2, cw), lambda i: (0, i)),
            pl.BlockSpec((2, cw), lambda i: (0, i)),
        ],
        out_specs=pl.BlockSpec((chunk, 2, mb), lambda i: (i, 0, 0)),
        out_shape=jax.ShapeDtypeStruct((t // mb, 2, mb), I32),
        compiler_params=_cparams(("arbitrary",)),
        name="moe_slot_index",
    )(pstart, eid, rank)


def _moe(hp, eid, wts, rank, counts, x, mod3, wg, wu, wd, seq, split_rows):
    t = x.shape[0]
    n_exp = wg.shape[0]
    be = EXPERT_BLOCK
    cnt = counts[:, 0].astype(I32)
    pcnt = (cnt + be - 1) // be * be
    pends = jnp.cumsum(pcnt)
    pstart = pends - pcnt
    nblk = pl.cdiv(2 * t, be) + n_exp
    first_row = jnp.arange(nblk, dtype=I32) * be
    blk_e = jnp.minimum(jnp.sum((pends[None, :] <= first_row[:, None]).astype(I32), axis=1), n_exp - 1)
    n_used = (pends[-1:] // be).astype(I32)
    dest3 = _slot_index(pstart, eid, rank)
    xb = _dispatch(dest3, hp, nblk * be)
    yb = _experts(blk_e, n_used, xb, wg, wu, wd)
    if split_rows is None:
        return _combine(dest3, yb, x, wts, mod3, seq, 0, t)
    return (_combine(dest3, yb, x, wts, mod3, seq, 0, split_rows),
            _combine(dest3, yb, x, wts, mod3, seq, split_rows, t - split_rows))


def kernel(x_prompt, x_sample, c_prompt, c_sample, mix_norm, ffn_norm, ada_w, ada_b, ab_w_in, ab_w_out, q_norm, k_norm, attn_sink, hy_conv_w, hy_conv_b, hy_f_w1, hy_f_b1, hy_f_freq1, hy_f_w2, hy_f_b2, hy_f_freq2, hy_f_w3, hy_skip, cv_pw1_w, cv_pw1_b, cv_dw_w, cv_dw_b, cv_ln_g, cv_ln_b, cv_pw2_w, cv_pw2_b, router_w, router_bias, e_w_gate, e_w_up, e_w_down):
    bp, seq, d = x_prompt.shape
    bs = x_sample.shape[0]
    batch = bp + bs
    t = batch * seq
    depth = ada_w.shape[0]
    n_exp = router_w.shape[1]
    aw = N_HEADS * HEAD_DIM
    assert seq % ROW_BLOCK == 0 and seq >= 3 * ATTN_BLOCK and x_sample.shape[1] == seq

    xp2 = x_prompt.reshape(bp * seq, d)
    xs2 = x_sample.reshape(bs * seq, d)
    x = None
    c = jnp.concatenate([c_prompt, c_sample], axis=0)
    mod = _ada(c, ada_w, ada_b)

    rc, rsa, rsb = _rope_tables(seq)
    head = jnp.arange(aw) // HEAD_DIM
    bd = jnp.where(head[:, None] == head[None, :], 1.0 / HEAD_DIM, 0.0).astype(BF16)
    cm, sm = _dft_tables(seq)
    feat_pad = 40
    hwidth = hy_skip.shape[2]
    zf, window = _filter_tables(seq, hwidth, feat_pad)
    rw_pad = jnp.pad(router_w, ((0, 0), (0, LANES - n_exp)))
    rwh, rwl = _split_bf16(rw_pad)
    rb = jnp.broadcast_to(router_bias.astype(F32)[:, None], (n_exp, LANES))
    ar = jnp.arange(2 * ROW_BLOCK)
    tri = (ar[:, None] < ar[None, :]).astype(BF16)

    for i in range(depth):
        mod3 = mod[i].reshape(batch, 1, 6 * d)
        j = i // 2
        fgain = ffn_norm[i].reshape(1, d)
        mgain = mix_norm[i].reshape(1, d)
        if i % 2 == 0:
            hidden = hy_f_w1.shape[2]
            order = hy_skip.shape[1]
            w1 = jnp.pad(hy_f_w1[j], ((0, feat_pad - hy_f_w1.shape[1]), (0, 0)))
            w3 = hy_f_w3[j].reshape(hidden, order, 2, hwidth).transpose(1, 2, 0, 3)
            kc, ks, kn = _hyena_spectra(
                zf, w1, hy_f_b1[j].reshape(1, hidden), hy_f_freq1[j].reshape(1, hidden),
                hy_f_w2[j], hy_f_b2[j].reshape(1, hidden), hy_f_freq2[j].reshape(1, hidden),
                w3, window, cm, sm)
            w_in = ab_w_in[j]
            kvw = N_KV_HEADS * HEAD_DIM

            def dup_heads(wpart):
                w4 = wpart.reshape(d, N_KV_HEADS, 1, HEAD_DIM)
                return jnp.broadcast_to(w4, (d, N_KV_HEADS, 2, HEAD_DIM)).reshape(d, KV_DUP_WIDTH)

            w_in = jnp.concatenate([w_in[:, :aw], dup_heads(w_in[:, aw:aw + kvw]),
                                    dup_heads(w_in[:, aw + kvw:aw + 2 * kvw]),
                                    w_in[:, aw + 2 * kvw:]], axis=1).astype(BF16)
            xa, xb = (xp2, xs2) if x is None else (x[:bp * seq], x[bp * seq:])
            q, k, v, hy = _inproj(
                xa, xb, mod3, mgain, w_in,
                jnp.tile(q_norm[j], N_HEADS).reshape(1, aw),
                jnp.tile(k_norm[j], 2 * N_KV_HEADS).reshape(1, KV_DUP_WIDTH),
                bd, rc, rsa, rsb, seq)
            ya = _attention(q, k, v, attn_sink[j], batch, seq)
            z = _hyena(hy, hy_conv_w[j], hy_conv_b[j].reshape(1, -1), cm, sm, kc, ks, kn,
                       hy_skip[j], batch, seq)
            x, hp, eid, wts, rank, counts = _outproj(
                ya, z, xa, xb, mod3, ab_w_out[j].astype(BF16), fgain, rwh, rwl, rb, tri, seq)
        else:
            if x is None:
                x = jnp.concatenate([xp2, xs2], axis=0)
            a = _pw1(x, mod3, mgain, cv_pw1_w[j].astype(BF16), cv_pw1_b[j].reshape(1, -1), seq)
            x, hp, eid, wts, rank, counts = _conv_pw2(
                a, x, mod3, cv_dw_w[j], cv_dw_b[j].reshape(1, -1), cv_ln_g[j].reshape(1, -1),
                cv_ln_b[j].reshape(1, -1), cv_pw2_w[j].astype(BF16), cv_pw2_b[j].reshape(1, -1),
                fgain, rwh, rwl, rb, tri, batch, seq)
        last = i == depth - 1
        x = _moe(hp, eid, wts, rank, counts, x, mod3, e_w_gate[i], e_w_up[i], e_w_down[i], seq,
                 bp * seq if last else None)

    if depth == 0:
        return (x_prompt, x_sample)
    return (x[0].reshape(bp, seq, d), x[1].reshape(bs, seq, d))
```
